```python
import math
import jax
import jax.numpy as jnp
from jax import lax
import numpy as np

D_MODEL = 1024
BATCH = 8
SEQ = 2048
DEPTH = 4

GRID_W = 64
CTX_LEN = 256
RMS_EPS = 1e-6
N_MOD = 9
FFN_HIDDEN = 2816

GM_GROUPS = 4
GM_WIDTH = 512
GM_CHUNK = 128

DN_HEADS = 4
DN_HEAD_DIM = 128
DN_WIDTH = DN_HEADS * DN_HEAD_DIM
DN_CONV = 5
DN_CHUNK = 64

MLA_HEADS = 8
MLA_NOPE = 64
MLA_ROPE = 32
MLA_QK = MLA_NOPE + MLA_ROPE
MLA_V = 64
MLA_WIDTH = MLA_HEADS * MLA_V
MLA_Q_LORA = 384
MLA_KV_LORA = 256
Q_BLOCK = 128
ROPE_BASE = 10000.0

N_BRANCH = 3
IN_SPLITS = (2 * GM_WIDTH, 3 * DN_WIDTH, DN_WIDTH, 4 * DN_HEADS, MLA_Q_LORA, MLA_KV_LORA + MLA_ROPE, N_BRANCH * D_MODEL)
IN_WIDTH = sum(IN_SPLITS)

kernel_name = 'hybrid_gmlp_deltanet_mla_prefix_dit'


def rms_norm(x, w):
    x32 = x.astype(jnp.float32)
    y = x32 * lax.rsqrt(jnp.mean(x32 * x32, axis=-1, keepdims=True) + RMS_EPS)
    return y.astype(x.dtype) * w


def layer_norm(x, w):
    x32 = x.astype(jnp.float32)
    mu = jnp.mean(x32, axis=-1, keepdims=True)
    xc = x32 - mu
    y = xc * lax.rsqrt(jnp.mean(xc * xc, axis=-1, keepdims=True) + RMS_EPS)
    return y.astype(x.dtype) * w


def l2_normalize(x):
    return x * lax.rsqrt(jnp.sum(x * x, axis=-1, keepdims=True) + RMS_EPS)


def modulate(x, shift, scale):
    return x * (1.0 + scale) + shift


def swiglu(h, w_gu, w_down):
    g, u = jnp.split(h @ w_gu, 2, axis=-1)
    return (jax.nn.silu(g) * u) @ w_down


def ada_ffn_half(h, shift, scale, gate, norm_w, w_gu, w_down):
    return h + 0.5 * gate * swiglu(modulate(rms_norm(h, norm_w), shift, scale), w_gu, w_down)


def split_in(p):
    offsets = np.cumsum(IN_SPLITS)[:-1].tolist()
    return jnp.split(p, offsets, axis=-1)


def axial_rope_tables(n_tokens, dtype):
    rows = n_tokens // GRID_W
    row = jnp.repeat(jnp.arange(rows), GRID_W).astype(jnp.float32)
    col = jnp.tile(jnp.arange(GRID_W), rows).astype(jnp.float32)
    half = MLA_ROPE // 2
    inv_freq = 1.0 / (ROPE_BASE ** (jnp.arange(0, half, 2, dtype=jnp.float32) / half))
    ang_r = row[:, None] * inv_freq
    ang_c = col[:, None] * inv_freq
    ang = jnp.concatenate([ang_r, ang_r, ang_c, ang_c], axis=-1)
    return jnp.cos(ang).astype(dtype), jnp.sin(ang).astype(dtype)


def rotate_half(y):
    y1, y2 = jnp.split(y, 2, axis=-1)
    return jnp.concatenate([-y2, y1], axis=-1)


def apply_axial_rope(t, cos, sin):
    tr, tc = jnp.split(t, 2, axis=-1)
    rot = jnp.concatenate([rotate_half(tr), rotate_half(tc)], axis=-1)
    return t * cos[None, :, None, :] + rot * sin[None, :, None, :]


def rope_tail(t, rope):
    if rope is None:
        return t
    cos, sin = rope
    return jnp.concatenate([t[..., :MLA_NOPE], apply_axial_rope(t[..., MLA_NOPE:], cos, sin)], axis=-1)


def chunk_gmlp_branch(z, ln_w, w_s, b_s):
    z = jax.nn.gelu(z)
    u, v = jnp.split(z, 2, axis=-1)
    v = layer_norm(v, ln_w)
    b, n, _ = v.shape
    v = v.reshape(b, n // GM_CHUNK, GM_CHUNK, GM_GROUPS, GM_WIDTH // GM_GROUPS)
    s = jnp.einsum('gpq,bnqgc->bnpgc', w_s, v) + b_s.T[None, None, :, :, None]
    return u * s.reshape(b, n, GM_WIDTH)


def centred_depthwise_conv(x, w):
    ch = x.shape[-1]
    return lax.conv_general_dilated(
        x, w[:, None, :].astype(x.dtype), window_strides=(1,),
        padding=[(DN_CONV // 2, DN_CONV // 2)],
        dimension_numbers=('NWC', 'WIO', 'NWC'), feature_group_count=ch)


def delta_prepare(qkv, ab, conv_w, a_log, dt_bias):
    b, n, _ = qkv.shape
    qkv = jax.nn.silu(centred_depthwise_conv(qkv, conv_w)).astype(jnp.float32)
    q, k, v = jnp.split(qkv, 3, axis=-1)

    def heads(y):
        return y.reshape(b, n, DN_HEADS, DN_HEAD_DIM).transpose(0, 2, 1, 3)

    q = l2_normalize(heads(q))
    k = l2_normalize(heads(k))
    v = heads(v)
    ab = ab.astype(jnp.float32).reshape(b, n, 2, 2, DN_HEADS)
    a, bpre = ab[:, :, 0], ab[:, :, 1]
    g = -jnp.exp(a_log.astype(jnp.float32)) * jax.nn.softplus(a + dt_bias.astype(jnp.float32))
    beta = jax.nn.sigmoid(bpre)
    return q, k, v, g.transpose(2, 0, 3, 1), beta.transpose(2, 0, 3, 1)


def gated_delta_chunked(q, k, v, g, beta, s0):
    b, h, n, dk = q.shape
    dv = v.shape[-1]
    nc = n // DN_CHUNK
    q = (q * dk ** -0.5).reshape(b, h, nc, DN_CHUNK, dk)
    k = k.reshape(b, h, nc, DN_CHUNK, dk)
    v = v.reshape(b, h, nc, DN_CHUNK, dv)
    beta = beta.reshape(b, h, nc, DN_CHUNK)
    gam = jnp.cumsum(g.reshape(b, h, nc, DN_CHUNK), axis=-1)
    incl = jnp.tril(jnp.ones((DN_CHUNK, DN_CHUNK), dtype=bool))
    strict = jnp.tril(jnp.ones((DN_CHUNK, DN_CHUNK), dtype=bool), k=-1)
    diff = gam[..., :, None] - gam[..., None, :]
    decay = jnp.where(incl, jnp.exp(jnp.where(incl, diff, 0.0)), 0.0)
    kb = k * beta[..., None]
    m = jnp.where(strict, jnp.einsum('bhncd,bhnsd->bhncs', kb, k) * decay, 0.0)
    eye = jnp.eye(DN_CHUNK, dtype=jnp.float32)
    t_inv = lax.linalg.triangular_solve(eye + m, jnp.broadcast_to(eye, m.shape), left_side=True, lower=True)
    u = jnp.einsum('bhncs,bhnse->bhnce', t_inv, v * beta[..., None])
    w = jnp.einsum('bhncs,bhnsd->bhncd', t_inv, kb * jnp.exp(gam)[..., None])
    qk = jnp.einsum('bhncd,bhnsd->bhncs', q, k) * decay

    def step(s, xs):
        q_c, k_c, u_c, w_c, qk_c, gam_c = xs
        v_new = u_c - jnp.einsum('bhcd,bhde->bhce', w_c, s)
        o = (jnp.einsum('bhcd,bhde->bhce', q_c * jnp.exp(gam_c)[..., None], s)
             + jnp.einsum('bhcs,bhse->bhce', qk_c, v_new))
        g_last = gam_c[..., -1]
        s = (s * jnp.exp(g_last)[..., None, None]
             + jnp.einsum('bhcd,bhce->bhde', k_c * jnp.exp(g_last[..., None] - gam_c)[..., None], v_new))
        return s, o

    xs = tuple(jnp.moveaxis(a, 2, 0) for a in (q, k, u, w, qk, gam))
    s_final, o = lax.scan(step, s0, xs)
    o = jnp.moveaxis(o, 0, 2).reshape(b, h, n, dv)
    return o, s_final


def _orient(t, reverse):
    return jnp.flip(t, axis=2) if reverse else t


def bidirectional_delta(qkv_c, ab_c, qkv_x, ab_x, conv_w, a_log, dt_bias):
    qc, kc, vc, gc, bc = delta_prepare(qkv_c, ab_c, conv_w, a_log, dt_bias)
    qx, kx, vx, gx, bx = delta_prepare(qkv_x, ab_x, conv_w, a_log, dt_bias)
    s_zero = jnp.zeros((qx.shape[0], DN_HEADS, DN_HEAD_DIM, DN_HEAD_DIM), jnp.float32)
    outs_c, outs_x = [], []
    for d in range(2):
        rev = d == 1
        oc, s_ctx = gated_delta_chunked(_orient(qc, rev), _orient(kc, rev), _orient(vc, rev),
                                        _orient(gc[d], rev), _orient(bc[d], rev), s_zero)
        ox, _ = gated_delta_chunked(_orient(qx, rev), _orient(kx, rev), _orient(vx, rev),
                                    _orient(gx[d], rev), _orient(bx[d], rev), s_ctx)
        outs_c.append(_orient(oc, rev))
        outs_x.append(_orient(ox, rev))
    return outs_c[0] + outs_c[1], outs_x[0] + outs_x[1]


def delta_output(o, z, norm_w):
    b, _, n, _ = o.shape
    o = rms_norm(o.transpose(0, 2, 1, 3), norm_w).astype(z.dtype)
    return (o * jax.nn.silu(z.reshape(b, n, DN_HEADS, DN_HEAD_DIM))).reshape(b, n, DN_WIDTH)


def mla_q(cq, q_norm, w_uq, qk_norm_q, rope):
    b, n, _ = cq.shape
    q = (rms_norm(cq, q_norm) @ w_uq).reshape(b, n, MLA_HEADS, MLA_QK)
    return rope_tail(rms_norm(q, qk_norm_q), rope)


def mla_kv(ckv_in, kv_norm, w_ukv, qk_norm_k, rope):
    b, n, _ = ckv_in.shape
    ckv, k_rope = jnp.split(ckv_in, [MLA_KV_LORA], axis=-1)
    kv = (rms_norm(ckv, kv_norm) @ w_ukv).reshape(b, n, MLA_HEADS, MLA_NOPE + MLA_V)
    k_nope, v = jnp.split(kv, [MLA_NOPE], axis=-1)
    k = jnp.concatenate([k_nope, jnp.broadcast_to(k_rope[:, :, None, :], (b, n, MLA_HEADS, MLA_ROPE))], axis=-1)
    return rope_tail(rms_norm(k, qk_norm_k), rope), v


def attend(q, k, v):
    s = jnp.einsum('bqhd,bkhd->bhqk', q, k, preferred_element_type=jnp.float32) * MLA_QK ** -0.5
    p = jax.nn.softmax(s, axis=-1).astype(v.dtype)
    return jnp.einsum('bhqk,bkhd->bqhd', p, v)


def latent_attention(q, k, v, k_ctx, v_ctx):
    b, n, h, d = q.shape
    k_all = jnp.concatenate([k, k_ctx], axis=1)
    v_all = jnp.concatenate([v, v_ctx], axis=1)
    qb = q.reshape(b, n // Q_BLOCK, Q_BLOCK, h, d).transpose(1, 0, 2, 3, 4)
    o = lax.map(lambda qq: attend(qq, k_all, v_all), qb)
    return o.transpose(1, 0, 2, 3, 4).reshape(b, n, MLA_WIDTH)


def merge_branches(ya, yb, yc, gates, w_a, w_b, w_c, w_o):
    ga, gb, gc = jnp.split(gates, N_BRANCH, axis=-1)
    m = (jax.nn.sigmoid(ga) * (ya @ w_a) + jax.nn.sigmoid(gb) * (yb @ w_b)
         + jax.nn.sigmoid(gc) * (yc @ w_c))
    return m @ w_o


def token_mixing(uc, ux, rope, w_in, gm_ln, gm_ws, gm_bs, dn_conv, dn_a_log, dn_dt_bias, dn_out_norm,
                 mla_q_norm, mla_w_uq, mla_kv_norm, mla_w_ukv, mla_qk_norm_q, mla_qk_norm_k,
                 w_branch_gm, w_branch_dn, w_branch_mla, w_out, need_ctx):
    gm_c, qkv_c, z_c, ab_c, cq_c, ckv_c, gate_c = split_in(uc @ w_in)
    gm_x, qkv_x, z_x, ab_x, cq_x, ckv_x, gate_x = split_in(ux @ w_in)
    ya_x = chunk_gmlp_branch(gm_x, gm_ln, gm_ws, gm_bs)
    o_c, o_x = bidirectional_delta(qkv_c, ab_c, qkv_x, ab_x, dn_conv, dn_a_log, dn_dt_bias)
    yb_x = delta_output(o_x, z_x, dn_out_norm)
    k_c, v_c = mla_kv(ckv_c, mla_kv_norm, mla_w_ukv, mla_qk_norm_k, None)
    k_x, v_x = mla_kv(ckv_x, mla_kv_norm, mla_w_ukv, mla_qk_norm_k, rope)
    q_x = mla_q(cq_x, mla_q_norm, mla_w_uq, mla_qk_norm_q, rope)
    yc_x = latent_attention(q_x, k_x, v_x, k_c, v_c)
    y_x = merge_branches(ya_x, yb_x, yc_x, gate_x, w_branch_gm, w_branch_dn, w_branch_mla, w_out)
    if not need_ctx:
        return None, y_x
    ya_c = chunk_gmlp_branch(gm_c, gm_ln, gm_ws, gm_bs)
    yb_c = delta_output(o_c, z_c, dn_out_norm)
    q_c = mla_q(cq_c, mla_q_norm, mla_w_uq, mla_qk_norm_q, None)
    yc_c = attend(q_c, k_c, v_c).reshape(q_c.shape[0], q_c.shape[1], MLA_WIDTH)
    y_c = merge_branches(ya_c, yb_c, yc_c, gate_c, w_branch_gm, w_branch_dn, w_branch_mla, w_out)
    return y_c, y_x


def setup_inputs(seed: int = 0) -> dict:
    key = jax.random.key(seed)
    keys = iter(jax.random.split(key, 48))
    f32 = jnp.float32
    L, D = DEPTH, D_MODEL

    def nrm(shape, scale):
        return jax.random.normal(next(keys), shape, f32) * scale

    def gain(shape):
        return 1.0 + 0.05 * jax.random.normal(next(keys), shape, f32)

    dt = jnp.exp(jax.random.uniform(next(keys), (L, 2, DN_HEADS), f32, math.log(1e-3), math.log(1e-1)))
    a_log = jnp.log(jax.random.uniform(next(keys), (L, 2, DN_HEADS), f32, 1.0, 16.0))
    return {
        'x': nrm((BATCH, SEQ, D), 1.0),
        'c': nrm((BATCH, D), 1.0),
        'ctx': nrm((BATCH, CTX_LEN, D), 1.0),
        'c_ctx': nrm((D,), 1.0),
        'ada_w': nrm((L, D, N_MOD * D), D ** -0.5),
        'ada_b': nrm((L, N_MOD * D), 0.02),
        'ffn1_norm': gain((L, D)),
        'ffn1_w_gu': nrm((L, D, 2 * FFN_HIDDEN), D ** -0.5),
        'ffn1_w_down': nrm((L, FFN_HIDDEN, D), FFN_HIDDEN ** -0.5),
        'mix_norm': gain((L, D)),
        'w_in': nrm((L, D, IN_WIDTH), D ** -0.5),
        'gm_ln': gain((L, GM_WIDTH)),
        'gm_ws': nrm((L, GM_GROUPS, GM_CHUNK, GM_CHUNK), GM_CHUNK ** -0.5),
        'gm_bs': 1.0 + nrm((L, GM_GROUPS, GM_CHUNK), 0.1),
        'dn_conv': nrm((L, DN_CONV, 3 * DN_WIDTH), DN_CONV ** -0.5),
        'dn_a_log': a_log,
        'dn_dt_bias': jnp.log(jnp.expm1(dt)),
        'dn_out_norm': gain((L, DN_HEAD_DIM)),
        'mla_q_norm': gain((L, MLA_Q_LORA)),
        'mla_w_uq': nrm((L, MLA_Q_LORA, MLA_HEADS * MLA_QK), MLA_Q_LORA ** -0.5),
        'mla_kv_norm': gain((L, MLA_KV_LORA)),
        'mla_w_ukv': nrm((L, MLA_KV_LORA, MLA_HEADS * (MLA_NOPE + MLA_V)), MLA_KV_LORA ** -0.5),
        'mla_qk_norm_q': gain((L, MLA_QK)),
        'mla_qk_norm_k': gain((L, MLA_QK)),
        'w_branch_gm': nrm((L, GM_WIDTH, D), GM_WIDTH ** -0.5),
        'w_branch_dn': nrm((L, DN_WIDTH, D), DN_WIDTH ** -0.5),
        'w_branch_mla': nrm((L, MLA_WIDTH, D), MLA_WIDTH ** -0.5),
        'w_out': nrm((L, D, D), D ** -0.5),
        'ffn2_norm': gain((L, D)),
        'ffn2_w_gu': nrm((L, D, 2 * FFN_HIDDEN), D ** -0.5),
        'ffn2_w_down': nrm((L, FFN_HIDDEN, D), FFN_HIDDEN ** -0.5),
    }


def reference(x, c, ctx, c_ctx, ada_w, ada_b, ffn1_norm, ffn1_w_gu, ffn1_w_down, mix_norm, w_in,
              gm_ln, gm_ws, gm_bs, dn_conv, dn_a_log, dn_dt_bias, dn_out_norm,
              mla_q_norm, mla_w_uq, mla_kv_norm, mla_w_ukv, mla_qk_norm_q, mla_qk_norm_k,
              w_branch_gm, w_branch_dn, w_branch_mla, w_out, ffn2_norm, ffn2_w_gu, ffn2_w_down):
    rope = axial_rope_tables(x.shape[1], x.dtype)
    hx, hc = x, ctx
    for l in range(DEPTH):
        need_ctx = l < DEPTH - 1
        mx = jnp.split((jax.nn.silu(c) @ ada_w[l] + ada_b[l])[:, None, :], N_MOD, axis=-1)
        mc = jnp.split(jax.nn.silu(c_ctx) @ ada_w[l] + ada_b[l], N_MOD, axis=-1)
        hc = ada_ffn_half(hc, mc[0], mc[1], mc[2], ffn1_norm[l], ffn1_w_gu[l], ffn1_w_down[l])
        hx = ada_ffn_half(hx, mx[0], mx[1], mx[2], ffn1_norm[l], ffn1_w_gu[l], ffn1_w_down[l])
        uc = modulate(rms_norm(hc, mix_norm[l]), mc[3], mc[4])
        ux = modulate(rms_norm(hx, mix_norm[l]), mx[3], mx[4])
        yc, yx = token_mixing(uc, ux, rope, w_in[l], gm_ln[l], gm_ws[l], gm_bs[l], dn_conv[l],
                              dn_a_log[l], dn_dt_bias[l], dn_out_norm[l], mla_q_norm[l], mla_w_uq[l],
                              mla_kv_norm[l], mla_w_ukv[l], mla_qk_norm_q[l], mla_qk_norm_k[l],
                              w_branch_gm[l], w_branch_dn[l], w_branch_mla[l], w_out[l], need_ctx)
        hx = hx + mx[5] * yx
        hx = ada_ffn_half(hx, mx[6], mx[7], mx[8], ffn2_norm[l], ffn2_w_gu[l], ffn2_w_down[l])
        if need_ctx:
            hc = hc + mc[5] * yc
            hc = ada_ffn_half(hc, mc[6], mc[7], mc[8], ffn2_norm[l], ffn2_w_gu[l], ffn2_w_down[l])
    return hx
```

```python
import functools
import math

import jax
import jax.numpy as jnp
from jax import lax
from jax.experimental import pallas as pl
from jax.experimental.pallas import tpu as pltpu

F32 = jnp.float32
BF16 = jnp.bfloat16

D_MODEL = 1024
DEPTH = 4
GRID_W = 64
CTX_LEN = 256
RMS_EPS = 1e-6
N_MOD = 9
FFN_HIDDEN = 2816

GM_GROUPS = 4
GM_WIDTH = 512
GM_CHUNK = 128

DN_HEADS = 4
DN_HEAD_DIM = 128
DN_WIDTH = DN_HEADS * DN_HEAD_DIM
DN_CONV = 5
DN_BLOCK = 128

MLA_HEADS = 8
MLA_NOPE = 64
MLA_ROPE = 32
MLA_QK = MLA_NOPE + MLA_ROPE
MLA_V = 64
MLA_WIDTH = MLA_HEADS * MLA_V
MLA_Q_LORA = 384
MLA_KV_LORA = 256
ROPE_BASE = 10000.0
N_BRANCH = 3

LANE = 128
HEAD_PAD = 128
MOD_ROWS = 16
AUX_W = 128
VMEM_LIMIT = 56 * 1024 * 1024

FFN_CHUNK = 256
ROW_TILE = 512
MIX_TILE = 256


def _cparams(n_axes):
    return pltpu.CompilerParams(dimension_semantics=("arbitrary",) * n_axes,
                                vmem_limit_bytes=VMEM_LIMIT)


def _dot(a, b):
    return jnp.dot(a, b, preferred_element_type=F32)


def _dot_nt(a, b):
    return lax.dot_general(a, b, (((1,), (1,)), ((), ())), preferred_element_type=F32)


def _sigmoid(x):
    return 1.0 / (1.0 + jnp.exp(-x))


def _silu(x):
    return x * _sigmoid(x)


def _gelu_tanh(x):
    return 0.5 * x * (1.0 + jnp.tanh(math.sqrt(2.0 / math.pi) * (x + 0.044715 * (x * x * x))))


def _const_spec(shape, single=True):
    nd = len(shape)
    kw = {"pipeline_mode": pl.Buffered(1)} if single else {}
    return pl.BlockSpec(shape, lambda *_: (0,) * nd, **kw)


def _layer_spec(layer, shape):
    nd = len(shape)
    return pl.BlockSpec((None,) + tuple(shape), lambda *_: (layer,) + (0,) * nd,
                        pipeline_mode=pl.Buffered(1))


def _mod_spec(layer):
    return pl.BlockSpec((None, None, N_MOD, D_MODEL), lambda s, *_: (layer, s, 0, 0))


def _ada_kernel(x_ref, w_ref, b_ref, o_ref):
    x = x_ref[...]
    xs = _silu(x).astype(BF16)
    o_ref[...] = _dot(xs, w_ref[...].astype(BF16)) + b_ref[...]


def _ada_table(cond, ada_w, ada_b):
    nb = D_MODEL
    out = pl.pallas_call(
        _ada_kernel,
        grid=(DEPTH, N_MOD),
        in_specs=[
            pl.BlockSpec((MOD_ROWS, D_MODEL), lambda l, j: (0, 0)),
            pl.BlockSpec((None, D_MODEL, nb), lambda l, j: (l, 0, j)),
            pl.BlockSpec((None, 1, nb), lambda l, j: (l, 0, j)),
        ],
        out_specs=pl.BlockSpec((None, MOD_ROWS, nb), lambda l, j: (l, 0, j)),
        out_shape=jax.ShapeDtypeStruct((DEPTH, MOD_ROWS, N_MOD * D_MODEL), F32),
        compiler_params=_cparams(2),
        name="ada_table",
    )(cond, ada_w, ada_b.reshape(DEPTH, 1, N_MOD * D_MODEL))
    return out.reshape(DEPTH, MOD_ROWS, N_MOD, D_MODEL)


def _mod_rmsnorm(h, norm_w, shift, scale):
    ms = jnp.mean(h * h, axis=-1, keepdims=True)
    xn = h * lax.rsqrt(ms + RMS_EPS) * norm_w
    return xn * (1.0 + scale) + shift


def _ffn_kernel(h_ref, mod_ref, nw_ref, wgu_ref, wd_ref, o_ref, *, k0):
    h = h_ref[...]
    xm = _mod_rmsnorm(h, nw_ref[...], mod_ref[k0:k0 + 1, :], mod_ref[k0 + 1:k0 + 2, :]).astype(BF16)
    acc = jnp.zeros(h.shape, F32)
    for c in range(FFN_HIDDEN // FFN_CHUNK):
        lo = c * FFN_CHUNK
        g = _dot(xm, wgu_ref[:, lo:lo + FFN_CHUNK])
        u = _dot(xm, wgu_ref[:, FFN_HIDDEN + lo:FFN_HIDDEN + lo + FFN_CHUNK])
        a = (_silu(g) * u).astype(BF16)
        acc = acc + _dot(a, wd_ref[lo:lo + FFN_CHUNK, :])
    o_ref[...] = h + 0.5 * mod_ref[k0 + 2:k0 + 3, :] * acc


def _ffn_half(h, mods, norm_w, w_gu, w_down, layer, k0, n_samples):
    ns, seq, d = h.shape
    tm = min(ROW_TILE, seq)
    tok = pl.BlockSpec((None, tm, d), lambda s, t: (s, t, 0))
    return pl.pallas_call(
        functools.partial(_ffn_kernel, k0=k0),
        grid=(n_samples, seq // tm),
        in_specs=[tok, _mod_spec(layer), _layer_spec(layer, (1, d)),
                  _layer_spec(layer, (d, 2 * FFN_HIDDEN)), _layer_spec(layer, (FFN_HIDDEN, d))],
        out_specs=tok,
        out_shape=jax.ShapeDtypeStruct((n_samples, seq, d), F32),
        compiler_params=_cparams(2),
        name=f"ffn_half_k{k0}",
    )(h, mods, norm_w.reshape(DEPTH, 1, d), w_gu, w_down)


_C_GM = 0
_C_QKV = _C_GM + 2 * GM_WIDTH
_C_Z = _C_QKV + 3 * DN_WIDTH
_C_CQ = _C_Z + DN_WIDTH
_C_CKV = _C_CQ + MLA_Q_LORA
_C_GATE = _C_CKV + MLA_KV_LORA
_C_AB = _C_GATE + N_BRANCH * D_MODEL
_C_KR = _C_AB + AUX_W
_C_END = _C_KR + HEAD_PAD


def _pack_w_in(w_in):
    splits = (2 * GM_WIDTH, 3 * DN_WIDTH, DN_WIDTH, 4 * DN_HEADS, MLA_Q_LORA, MLA_KV_LORA + MLA_ROPE,
              N_BRANCH * D_MODEL)
    offs = [0]
    for s in splits:
        offs.append(offs[-1] + s)
    gm, qkv, z, ab, cq, ckvr, gate = (w_in[..., offs[i]:offs[i + 1]] for i in range(7))
    ckv, kr = ckvr[..., :MLA_KV_LORA], ckvr[..., MLA_KV_LORA:]
    lead = w_in.shape[:-1]
    ab_p = jnp.concatenate([ab, jnp.zeros(lead + (AUX_W - 4 * DN_HEADS,), w_in.dtype)], -1)
    kr_p = jnp.concatenate([jnp.zeros(lead + (MLA_NOPE,), w_in.dtype), kr,
                            jnp.zeros(lead + (HEAD_PAD - MLA_QK,), w_in.dtype)], -1)
    return jnp.concatenate([gm, qkv, z, cq, ckv, gate, ab_p, kr_p], -1).astype(BF16)


def _mixer_in_kernel(h_ref, mod_ref, nw_ref, w_ref, ln_ref, ws_ref, bs_ref,
                     ya_ref, qkv_ref, z_ref, cq_ref, ckv_ref, gate_ref, ab_ref, kr_ref):
    h = h_ref[...]
    um = _mod_rmsnorm(h, nw_ref[...], mod_ref[3:4, :], mod_ref[4:5, :]).astype(BF16)
    qkv_ref[...] = _dot(um, w_ref[:, _C_QKV:_C_Z])
    z_ref[...] = _dot(um, w_ref[:, _C_Z:_C_CQ])
    cq_ref[...] = _dot(um, w_ref[:, _C_CQ:_C_CKV])
    ckv_ref[...] = _dot(um, w_ref[:, _C_CKV:_C_GATE])
    gate_ref[...] = _dot(um, w_ref[:, _C_GATE:_C_AB])
    ab_ref[...] = _dot(um, w_ref[:, _C_AB:_C_KR])
    kr_ref[...] = _dot(um, w_ref[:, _C_KR:_C_END])
    zz = _gelu_tanh(_dot(um, w_ref[:, _C_GM:_C_QKV]))
    uu = zz[:, :GM_WIDTH]
    v = zz[:, GM_WIDTH:]
    mu = jnp.mean(v, axis=-1, keepdims=True)
    vc = v - mu
    var = jnp.mean(vc * vc, axis=-1, keepdims=True)
    vn = (vc * lax.rsqrt(var + RMS_EPS) * ln_ref[...]).astype(BF16)
    gw = GM_WIDTH // GM_GROUPS
    for c in range(h.shape[0] // GM_CHUNK):
        r = slice(c * GM_CHUNK, (c + 1) * GM_CHUNK)
        for g in range(GM_GROUPS):
            cs = slice(g * gw, (g + 1) * gw)
            s = _dot(ws_ref[g], vn[r, cs]) + bs_ref[:, g:g + 1]
            ya_ref[r, cs] = (uu[r, cs] * s).astype(ya_ref.dtype)


def _mixer_in(h, mods, norm_w, w_packed, gm_ln, gm_ws, gm_bs_t, layer):
    ns, seq, d = h.shape
    tm = min(MIX_TILE, seq)

    def tok(width):
        return pl.BlockSpec((None, tm, width), lambda s, t: (s, t, 0))

    def out(width, dtype=F32):
        return jax.ShapeDtypeStruct((ns, seq, width), dtype)

    return pl.pallas_call(
        _mixer_in_kernel,
        grid=(ns, seq // tm),
        in_specs=[tok(d), _mod_spec(layer), _layer_spec(layer, (1, d)), _layer_spec(layer, (d, _C_END)),
                  _layer_spec(layer, (1, GM_WIDTH)),
                  _layer_spec(layer, (GM_GROUPS, GM_CHUNK, GM_CHUNK)),
                  _layer_spec(layer, (GM_CHUNK, GM_GROUPS))],
        out_specs=[tok(GM_WIDTH), tok(3 * DN_WIDTH), tok(DN_WIDTH), tok(MLA_Q_LORA), tok(MLA_KV_LORA),
                   tok(N_BRANCH * d), tok(AUX_W), tok(HEAD_PAD)],
        out_shape=[out(GM_WIDTH, BF16), out(3 * DN_WIDTH), out(DN_WIDTH), out(MLA_Q_LORA),
                   out(MLA_KV_LORA), out(N_BRANCH * d), out(AUX_W), out(HEAD_PAD)],
        compiler_params=_cparams(2),
        name="mixer_in",
    )(h, mods, norm_w.reshape(DEPTH, 1, d), w_packed, gm_ln.reshape(DEPTH, 1, GM_WIDTH), gm_ws, gm_bs_t)


def _tri(n, upper, inclusive=True):
    r = lax.broadcasted_iota(jnp.int32, (n, n), 0)
    c = lax.broadcasted_iota(jnp.int32, (n, n), 1)
    if upper:
        return (r <= c) if inclusive else (r < c)
    return (r >= c) if inclusive else (r > c)


def _delta_aux_kernel(ab_ref, alog_ref, dtb_ref, a_ref, at_ref):
    nh2 = 2 * DN_HEADS
    ab = ab_ref[...]
    sp = jnp.maximum(ab + dtb_ref[...], 0.0) + jnp.log1p(jnp.exp(-jnp.abs(ab + dtb_ref[...])))
    g = -jnp.exp(alog_ref[...]) * sp
    lane = lax.broadcasted_iota(jnp.int32, (DN_BLOCK, AUX_W), 1)
    tri_lo = _tri(DN_BLOCK, False).astype(F32)
    tri_up = _tri(DN_BLOCK, True).astype(F32)
    ones = jnp.ones((DN_BLOCK, DN_BLOCK), F32)
    beta = _sigmoid(ab)
    for c in range(ab.shape[0] // DN_BLOCK):
        r = slice(c * DN_BLOCK, (c + 1) * DN_BLOCK)
        gb = jnp.where(lane < nh2, g[r], 0.0)
        cf = jnp.dot(tri_lo, gb, preferred_element_type=F32, precision=lax.Precision.HIGHEST)
        cr = jnp.dot(tri_up, gb, preferred_element_type=F32, precision=lax.Precision.HIGHEST)
        tot = jnp.dot(ones, gb, preferred_element_type=F32, precision=lax.Precision.HIGHEST)
        gam = jnp.where(lane < DN_HEADS, cf, cr)
        tot_sh = pltpu.roll(tot, 2 * nh2, 1)
        blk = jnp.where(lane < nh2, gam, jnp.where(lane < 2 * nh2, beta[r], jnp.where(lane < 3 * nh2, tot_sh, 0.0)))
        a_ref[r, :] = blk
        at_ref[c] = blk.T[:4 * nh2, :]


def _delta_aux(ab, a_log, dt_bias, layer):
    ns, seq, _ = ab.shape
    nblk = seq // DN_BLOCK
    nh2 = 2 * DN_HEADS

    def row(p):
        return jnp.concatenate([p.reshape(DEPTH, 1, nh2), jnp.zeros((DEPTH, 1, AUX_W - nh2), F32)], -1)

    return pl.pallas_call(
        _delta_aux_kernel,
        grid=(ns,),
        in_specs=[pl.BlockSpec((None, seq, AUX_W), lambda s: (s, 0, 0)),
                  _layer_spec(layer, (1, AUX_W)), _layer_spec(layer, (1, AUX_W))],
        out_specs=[pl.BlockSpec((None, seq, AUX_W), lambda s: (s, 0, 0)),
                   pl.BlockSpec((None, nblk, 4 * nh2, DN_BLOCK), lambda s: (s, 0, 0, 0))],
        out_shape=[jax.ShapeDtypeStruct((ns, seq, AUX_W), F32),
                   jax.ShapeDtypeStruct((ns, nblk, 4 * nh2, DN_BLOCK), F32)],
        compiler_params=_cparams(1),
        name="delta_aux",
    )(ab, row(a_log), row(dt_bias))


_CONV_SUB = 256
_CONV_HALO = 8


def _delta_conv_kernel(x_ref, w_ref, o_ref, xp_ref, *, seq_ctx):
    s = pl.program_id(0)
    j = pl.program_id(1)
    seq = x_ref.shape[0]
    width = x_ref.shape[1]
    zeros = jnp.zeros((_CONV_HALO, width), F32)
    xp_ref[0:_CONV_HALO, :] = zeros
    xp_ref[_CONV_HALO + seq:, :] = zeros
    xp_ref[_CONV_HALO:_CONV_HALO + seq, :] = x_ref[...]
    is_ctx = s == pl.num_programs(0) - 1
    seq_mask = jnp.where(is_ctx, seq_ctx - 1, seq - 1)
    seq_len = seq_mask + 1
    pad = DN_CONV // 2
    for t in range(seq // _CONV_SUB):
        r0 = t * _CONV_SUB
        pos = (lax.broadcasted_iota(jnp.int32, (_CONV_SUB, 1), 0) + r0) & seq_mask
        acc = jnp.zeros((_CONV_SUB, width), F32)
        for k in range(DN_CONV):
            src = pos + (k - pad)
            ok = (src >= 0) & (src < seq_len)
            xs = xp_ref[_CONV_HALO + r0 + k - pad:_CONV_HALO + r0 + k - pad + _CONV_SUB, :]
            acc = acc + jnp.where(ok, xs, 0.0) * w_ref[k:k + 1, :]
        y = _silu(acc)
        for hd in range(width // DN_HEAD_DIM):
            cs = slice(hd * DN_HEAD_DIM, (hd + 1) * DN_HEAD_DIM)
            yh = y[:, cs]
            inv = lax.rsqrt(jnp.sum(yh * yh, axis=-1, keepdims=True) + RMS_EPS)
            fac = jnp.where(j == 0, inv * DN_HEAD_DIM ** -0.5, jnp.where(j == 1, inv, 1.0))
            o_ref[r0:r0 + _CONV_SUB, cs] = yh * fac


def _delta_conv(qkv, conv_w, layer, seq_ctx):
    ns, seq, _ = qkv.shape
    blk = pl.BlockSpec((None, seq, DN_WIDTH), lambda s, j: (s, 0, j))
    return pl.pallas_call(
        functools.partial(_delta_conv_kernel, seq_ctx=seq_ctx),
        grid=(ns, 3),
        in_specs=[blk, pl.BlockSpec((None, DN_CONV, DN_WIDTH), lambda s, j: (layer, 0, j))],
        out_specs=blk,
        out_shape=jax.ShapeDtypeStruct((ns, seq, 3 * DN_WIDTH), F32),
        scratch_shapes=[pltpu.VMEM((seq + 2 * _CONV_HALO, DN_WIDTH), F32)],
        compiler_params=_cparams(2),
        name="delta_conv",
    )(qkv, conv_w)


def _unit_tri_inverse(m, upper):
    n = m.shape[0]
    r = lax.broadcasted_iota(jnp.int32, (n, n), 0)
    c = lax.broadcasted_iota(jnp.int32, (n, n), 1)
    x = r ^ c
    eye = (r == c).astype(F32)
    t = eye - jnp.where(x < 2, m, 0.0)
    s = 2
    while s < n:
        off = jnp.where((x >= s) & (x < 2 * s), m, 0.0).astype(BF16)
        tb = t.astype(BF16)
        t = t - _dot(tb, _dot(off, tb).astype(BF16))
        s *= 2
    return t


def _delta_block(q, k, v, ablk, atblk, s_prev, d, hd):
    c = d * DN_HEADS + hd
    nh2 = 2 * DN_HEADS
    upper = d == 1
    gam_c = ablk[:, c:c + 1]
    beta_c = ablk[:, nh2 + c:nh2 + c + 1]
    tot_c = ablk[:, 2 * nh2 + c:2 * nh2 + c + 1]
    gam_r = atblk[c:c + 1, :]
    tot_r = atblk[2 * nh2 + c:2 * nh2 + c + 1, :]
    incl = _tri(DN_BLOCK, upper, True)
    strict = _tri(DN_BLOCK, upper, False)
    decay = jnp.where(incl, jnp.exp(jnp.where(incl, gam_c - gam_r, 0.0)), 0.0)
    kb = k * beta_c
    kf = k.astype(BF16)
    m = jnp.where(strict, _dot_nt(kb.astype(BF16), kf) * decay, 0.0)
    t_inv = _unit_tri_inverse(m, upper).astype(BF16)
    rhs = jnp.concatenate([v * beta_c, kb * jnp.exp(gam_c)], axis=1).astype(BF16)
    uw = _dot(t_inv, rhs)
    u = uw[:, :DN_HEAD_DIM]
    w = uw[:, DN_HEAD_DIM:]
    qk = (_dot_nt(q.astype(BF16), kf) * decay).astype(BF16)
    sb = s_prev.astype(BF16)
    lhs = jnp.concatenate([w, q * jnp.exp(gam_c)], axis=0).astype(BF16)
    ws_qs = _dot(lhs, sb)
    v_new = u - ws_qs[:DN_BLOCK]
    vnb = v_new.astype(BF16)
    o = ws_qs[DN_BLOCK:] + _dot(qk, vnb)
    kd_t = (k * jnp.exp(tot_c - gam_c)).T.astype(BF16)
    s_new = s_prev * jnp.exp(tot_r) + _dot(kd_t, vnb)
    return o, s_new


def _delta_kernel(qx_ref, kx_ref, vx_ref, ax_ref, atx_ref, qc_ref, kc_ref, vc_ref, ac_ref, atc_ref,
                  ox_ref, oc_ref, s_ref):
    s_ref[...] = jnp.zeros(s_ref.shape, F32)
    ox_ref[...] = jnp.zeros(ox_ref.shape, F32)
    oc_ref[...] = jnp.zeros(oc_ref.shape, F32)

    def scan(q_ref, k_ref, v_ref, a_ref, at_ref, o_ref):
        nblk = o_ref.shape[0] // DN_BLOCK

        def body(i, carry):
            for d in range(2):
                bi = i if d == 0 else nblk - 1 - i
                rows = pl.ds(pl.multiple_of(bi * DN_BLOCK, DN_BLOCK), DN_BLOCK)
                ablk = a_ref[rows, :]
                atblk = at_ref[bi]
                for hd in range(DN_HEADS):
                    cs = slice(hd * DN_HEAD_DIM, (hd + 1) * DN_HEAD_DIM)
                    o, s_new = _delta_block(q_ref[rows, cs], k_ref[rows, cs], v_ref[rows, cs], ablk, atblk,
                                            s_ref[d * DN_HEADS + hd], d, hd)
                    s_ref[d * DN_HEADS + hd] = s_new
                    o_ref[rows, cs] += o
            return carry
        lax.fori_loop(0, nblk, body, 0)

    scan(qc_ref, kc_ref, vc_ref, ac_ref, atc_ref, oc_ref)
    scan(qx_ref, kx_ref, vx_ref, ax_ref, atx_ref, ox_ref)


def _delta_rule(qkvn, aux, aux_t, batch, seq_ctx):
    ns, seq, _ = qkvn.shape
    nblk = seq // DN_BLOCK
    nh2 = 2 * DN_HEADS

    def lat(j):
        return pl.BlockSpec((None, seq, DN_WIDTH), lambda b: (b, 0, j))

    def ctx(j):
        return pl.BlockSpec((None, seq_ctx, DN_WIDTH), lambda b: (batch, b, j))

    return pl.pallas_call(
        _delta_kernel,
        grid=(batch,),
        in_specs=[lat(0), lat(1), lat(2),
                  pl.BlockSpec((None, seq, AUX_W), lambda b: (b, 0, 0)),
                  pl.BlockSpec((None, nblk, 4 * nh2, DN_BLOCK), lambda b: (b, 0, 0, 0)),
                  ctx(0), ctx(1), ctx(2),
                  pl.BlockSpec((None, seq_ctx, AUX_W), lambda b: (batch, b, 0)),
                  pl.BlockSpec((None, seq_ctx // DN_BLOCK, 4 * nh2, DN_BLOCK), lambda b: (batch, b, 0, 0))],
        out_specs=[pl.BlockSpec((None, seq, DN_WIDTH), lambda b: (b, 0, 0)),
                   pl.BlockSpec((None, seq_ctx, DN_WIDTH), lambda b: (b, 0, 0))],
        out_shape=[jax.ShapeDtypeStruct((batch, seq, DN_WIDTH), F32),
                   jax.ShapeDtypeStruct((batch, seq_ctx, DN_WIDTH), F32)],
        scratch_shapes=[pltpu.VMEM((nh2, DN_HEAD_DIM, DN_HEAD_DIM), F32)],
        compiler_params=_cparams(1),
        name="delta_rule",
    )(qkvn, qkvn, qkvn, aux, aux_t, qkvn, qkvn, qkvn, aux, aux_t)


def _rope_tables(seq):
    rows = seq // GRID_W
    row = jnp.repeat(jnp.arange(rows), GRID_W).astype(F32)
    col = jnp.tile(jnp.arange(GRID_W), rows).astype(F32)
    half = MLA_ROPE // 2
    inv_freq = 1.0 / (ROPE_BASE ** (jnp.arange(0, half, 2, dtype=F32) / half))
    ang_r = row[:, None] * inv_freq
    ang_c = col[:, None] * inv_freq
    ang = jnp.concatenate([ang_r, ang_r, ang_c, ang_c], axis=-1)
    cos, sin = jnp.cos(ang), jnp.sin(ang)
    q4 = half // 2
    z = jnp.zeros((seq, q4), F32)
    sin_up = jnp.concatenate([-sin[:, 0:q4], z, -sin[:, half:half + q4], z], -1)
    sin_dn = jnp.concatenate([z, sin[:, q4:half], z, sin[:, half + q4:]], -1)

    def pad(t, fill):
        return jnp.concatenate([jnp.full((seq, MLA_NOPE), fill, F32), t,
                                jnp.full((seq, HEAD_PAD - MLA_QK), fill, F32)], -1)

    lat = jnp.stack([pad(cos, 1.0), pad(sin_up, 0.0), pad(sin_dn, 0.0)])
    ident = jnp.stack([jnp.ones((seq, HEAD_PAD), F32), jnp.zeros((seq, HEAD_PAD), F32),
                       jnp.zeros((seq, HEAD_PAD), F32)])
    return jnp.stack([lat, ident])


def _head_norm_rope(xh, gain, cos, sin_up, sin_dn):
    ms = jnp.sum(xh * xh, axis=-1, keepdims=True) * (1.0 / MLA_QK)
    y = xh * lax.rsqrt(ms + RMS_EPS) * gain
    q4 = MLA_ROPE // 4
    return y * cos + pltpu.roll(y, HEAD_PAD - q4, 1) * sin_up + pltpu.roll(y, q4, 1) * sin_dn


def _mla_prep_kernel(cq_ref, ckv_ref, kr_ref, rope_ref, qn_ref, wuq_ref, kvn_ref, wuk_ref, wuv_ref,
                     gq_ref, gk_ref, q_ref, k_ref, v_ref):
    cos, sin_up, sin_dn = rope_ref[0], rope_ref[1], rope_ref[2]
    cq = cq_ref[...]
    cqn = (cq * lax.rsqrt(jnp.mean(cq * cq, axis=-1, keepdims=True) + RMS_EPS) * qn_ref[...]).astype(BF16)
    q = _dot(cqn, wuq_ref[...])
    ckv = ckv_ref[...]
    ckvn = (ckv * lax.rsqrt(jnp.mean(ckv * ckv, axis=-1, keepdims=True) + RMS_EPS) * kvn_ref[...]).astype(BF16)
    kn = _dot(ckvn, wuk_ref[...])
    v_ref[...] = _dot(ckvn, wuv_ref[...]).astype(v_ref.dtype)
    kr = kr_ref[...]
    for hd in range(MLA_HEADS):
        cs = slice(hd * HEAD_PAD, (hd + 1) * HEAD_PAD)
        qh = _head_norm_rope(q[:, cs], gq_ref[...], cos, sin_up, sin_dn)
        q_ref[:, cs] = (qh * MLA_QK ** -0.5).astype(q_ref.dtype)
        kh = _head_norm_rope(kn[:, cs] + kr, gk_ref[...], cos, sin_up, sin_dn)
        k_ref[:, cs] = kh.astype(k_ref.dtype)


def _pad_heads(w, valid, lo=0):
    lead = w.shape[:-1]
    w = w.reshape(lead + (MLA_HEADS, valid))
    parts = []
    if lo:
        parts.append(jnp.zeros(lead + (MLA_HEADS, lo), w.dtype))
    parts.append(w)
    parts.append(jnp.zeros(lead + (MLA_HEADS, HEAD_PAD - lo - valid), w.dtype))
    return jnp.concatenate(parts, -1).reshape(lead + (MLA_HEADS * HEAD_PAD,))


def _pad_gain(g):
    return jnp.concatenate([g, jnp.zeros(g.shape[:-1] + (HEAD_PAD - MLA_QK,), g.dtype)], -1)[:, None, :]


def _mla_prep(cq, ckv, kr, rope, q_norm, w_uq_p, kv_norm, w_uk_p, w_uv, gq, gk, layer):
    ns, seq, _ = cq.shape
    tm = min(ROW_TILE, seq)

    def tok(width):
        return pl.BlockSpec((None, tm, width), lambda s, t: (s, t, 0))

    wq = MLA_HEADS * HEAD_PAD
    return pl.pallas_call(
        _mla_prep_kernel,
        grid=(ns, seq // tm),
        in_specs=[tok(MLA_Q_LORA), tok(MLA_KV_LORA), tok(HEAD_PAD),
                  pl.BlockSpec((None, 3, tm, HEAD_PAD), lambda s, t: (s // (ns - 1), 0, t, 0)),
                  _layer_spec(layer, (1, MLA_Q_LORA)), _layer_spec(layer, (MLA_Q_LORA, wq)),
                  _layer_spec(layer, (1, MLA_KV_LORA)), _layer_spec(layer, (MLA_KV_LORA, wq)),
                  _layer_spec(layer, (MLA_KV_LORA, MLA_WIDTH)),
                  _layer_spec(layer, (1, HEAD_PAD)), _layer_spec(layer, (1, HEAD_PAD))],
        out_specs=[tok(wq), tok(wq), tok(MLA_WIDTH)],
        out_shape=[jax.ShapeDtypeStruct((ns, seq, wq), BF16), jax.ShapeDtypeStruct((ns, seq, wq), BF16),
                   jax.ShapeDtypeStruct((ns, seq, MLA_WIDTH), BF16)],
        compiler_params=_cparams(2),
        name="mla_prep",
    )(cq, ckv, kr, rope, q_norm.reshape(DEPTH, 1, MLA_Q_LORA), w_uq_p, kv_norm.reshape(DEPTH, 1, MLA_KV_LORA),
      w_uk_p, w_uv, gq, gk)


def _attn_kernel(*refs, n_src):
    q_ref = refs[0]
    k_refs = refs[1:1 + n_src]
    v_refs = refs[1 + n_src:1 + 2 * n_src]
    o_ref = refs[1 + 2 * n_src]
    lane = lax.broadcasted_iota(jnp.int32, o_ref.shape, 1)
    out = jnp.zeros(o_ref.shape, F32)
    for hh in range(2):
        cs = slice(hh * HEAD_PAD, (hh + 1) * HEAD_PAD)
        qh = q_ref[:, cs]
        scores = [_dot_nt(qh, k_ref[:, cs]) for k_ref in k_refs]
        m = scores[0].max(axis=-1, keepdims=True)
        for sc in scores[1:]:
            m = jnp.maximum(m, sc.max(axis=-1, keepdims=True))
        den = jnp.zeros_like(m)
        acc = jnp.zeros(o_ref.shape, F32)
        for sc, v_ref in zip(scores, v_refs):
            p = jnp.exp(sc - m)
            den = den + p.sum(axis=-1, keepdims=True)
            acc = acc + _dot(p.astype(BF16), v_ref[...])
        out = jnp.where((lane >= hh * MLA_V) & (lane < (hh + 1) * MLA_V), acc / den, out)
    o_ref[...] = out.astype(o_ref.dtype)


ATTN_Q_TILE = 512


def _attn_latent(q, k, v, batch, seq_ctx):
    ns, seq, _ = q.shape
    tq = min(ATTN_Q_TILE, seq)
    pair = 2 * HEAD_PAD
    return pl.pallas_call(
        functools.partial(_attn_kernel, n_src=2),
        grid=(batch, MLA_HEADS // 2, seq // tq),
        in_specs=[pl.BlockSpec((None, tq, pair), lambda b, h, t: (b, t, h)),
                  pl.BlockSpec((None, seq, pair), lambda b, h, t: (b, 0, h)),
                  pl.BlockSpec((None, seq_ctx, pair), lambda b, h, t: (batch, b, h)),
                  pl.BlockSpec((None, seq, 2 * MLA_V), lambda b, h, t: (b, 0, h)),
                  pl.BlockSpec((None, seq_ctx, 2 * MLA_V), lambda b, h, t: (batch, b, h))],
        out_specs=pl.BlockSpec((None, tq, 2 * MLA_V), lambda b, h, t: (b, t, h)),
        out_shape=jax.ShapeDtypeStruct((ns, seq, MLA_WIDTH), BF16),
        compiler_params=_cparams(3),
        name="attn_latent",
    )(q, k, k, v, v)


def _attn_context(q, k, v, yc, batch, seq_ctx):
    ns, seq, _ = q.shape
    pair = 2 * HEAD_PAD

    def kern(q_ref, k_ref, v_ref, yc_in_ref, o_ref):
        del yc_in_ref
        _attn_kernel(q_ref, k_ref, v_ref, o_ref, n_src=1)

    return pl.pallas_call(
        kern,
        grid=(batch, MLA_HEADS // 2),
        in_specs=[pl.BlockSpec((None, seq_ctx, pair), lambda b, h: (batch, b, h)),
                  pl.BlockSpec((None, seq_ctx, pair), lambda b, h: (batch, b, h)),
                  pl.BlockSpec((None, seq_ctx, 2 * MLA_V), lambda b, h: (batch, b, h)),
                  pl.BlockSpec(memory_space=pl.ANY)],
        out_specs=pl.BlockSpec((None, seq_ctx, 2 * MLA_V), lambda b, h: (batch, b, h)),
        out_shape=jax.ShapeDtypeStruct(yc.shape, yc.dtype),
        input_output_aliases={3: 0},
        compiler_params=_cparams(2),
        name="attn_context",
    )(q, k, v, yc)


def _merge_kernel(h_ref, mod_ref, ya_ref, o_ref, z_ref, yc_ref, gate_ref, onw_ref,
                  wa_ref, wb_ref, wc_ref, wo_ref, out_ref):
    d = h_ref.shape[1]
    o = o_ref[...]
    z = z_ref[...]
    yb_parts = []
    for hd in range(DN_HEADS):
        cs = slice(hd * DN_HEAD_DIM, (hd + 1) * DN_HEAD_DIM)
        oh = o[:, cs]
        ohn = oh * lax.rsqrt(jnp.mean(oh * oh, axis=-1, keepdims=True) + RMS_EPS) * onw_ref[...]
        yb_parts.append(ohn * _silu(z[:, cs]))
    yb = jnp.concatenate(yb_parts, axis=1).astype(BF16)
    m = (_sigmoid(gate_ref[:, 0:d]) * _dot(ya_ref[...], wa_ref[...])
         + _sigmoid(gate_ref[:, d:2 * d]) * _dot(yb, wb_ref[...])
         + _sigmoid(gate_ref[:, 2 * d:3 * d]) * _dot(yc_ref[...], wc_ref[...]))
    y = _dot(m.astype(BF16), wo_ref[...])
    out_ref[...] = h_ref[...] + mod_ref[5:6, :] * y


def _merge(h, mods, ya, o, z, yc, gate, out_norm, w_a, w_b, w_c, w_o, layer, n_samples):
    ns, seq, d = h.shape
    tm = min(ROW_TILE, seq)

    def tok(width):
        return pl.BlockSpec((None, tm, width), lambda s, t: (s, t, 0))

    return pl.pallas_call(
        _merge_kernel,
        grid=(n_samples, seq // tm),
        in_specs=[tok(d), _mod_spec(layer), tok(GM_WIDTH), tok(DN_WIDTH), tok(DN_WIDTH), tok(MLA_WIDTH),
                  tok(N_BRANCH * d), _layer_spec(layer, (1, DN_HEAD_DIM)),
                  _layer_spec(layer, (GM_WIDTH, d)), _layer_spec(layer, (DN_WIDTH, d)),
                  _layer_spec(layer, (MLA_WIDTH, d)), _layer_spec(layer, (d, d))],
        out_specs=tok(d),
        out_shape=jax.ShapeDtypeStruct((n_samples, seq, d), F32),
        compiler_params=_cparams(2),
        name="merge",
    )(h, mods, ya, o, z, yc, gate, out_norm.reshape(DEPTH, 1, DN_HEAD_DIM), w_a, w_b, w_c, w_o)


def kernel(x, c, ctx, c_ctx, ada_w, ada_b, ffn1_norm, ffn1_w_gu, ffn1_w_down, mix_norm, w_in, gm_ln, gm_ws, gm_bs, dn_conv, dn_a_log, dn_dt_bias, dn_out_norm, mla_q_norm, mla_w_uq, mla_kv_norm, mla_w_ukv, mla_qk_norm_q, mla_qk_norm_k, w_branch_gm, w_branch_dn, w_branch_mla, w_out, ffn2_norm, ffn2_w_gu, ffn2_w_down):
    batch, seq, d = x.shape
    seq_ctx = ctx.shape[1]
    assert d == D_MODEL and batch * seq_ctx == seq and batch + 1 <= MOD_ROWS
    assert seq % ROW_TILE == 0 or seq < ROW_TILE
    ns = batch + 1

    cond = jnp.concatenate([c, c_ctx[None, :], jnp.zeros((MOD_ROWS - ns, d), F32)], 0)
    mods = _ada_table(cond, ada_w, ada_b)
    h = jnp.concatenate([x, ctx.reshape(1, seq, d)], 0)

    bf = lambda w: w.astype(BF16)
    ffn1_gu, ffn1_dn, ffn2_gu, ffn2_dn = bf(ffn1_w_gu), bf(ffn1_w_down), bf(ffn2_w_gu), bf(ffn2_w_down)
    w_in_p = _pack_w_in(w_in)
    gm_ws_b = bf(gm_ws)
    gm_bs_t = jnp.swapaxes(gm_bs, 1, 2)
    w_uq_p = bf(_pad_heads(mla_w_uq, MLA_QK))
    ukv = mla_w_ukv.reshape(DEPTH, MLA_KV_LORA, MLA_HEADS, MLA_NOPE + MLA_V)
    w_uk_p = bf(_pad_heads(ukv[..., :MLA_NOPE].reshape(DEPTH, MLA_KV_LORA, MLA_HEADS * MLA_NOPE), MLA_NOPE))
    w_uv = bf(ukv[..., MLA_NOPE:].reshape(DEPTH, MLA_KV_LORA, MLA_WIDTH))
    gq, gk = _pad_gain(mla_qk_norm_q), _pad_gain(mla_qk_norm_k)
    w_a, w_b, w_c, w_o = bf(w_branch_gm), bf(w_branch_dn), bf(w_branch_mla), bf(w_out)
    rope = _rope_tables(seq)

    for l in range(DEPTH):
        need_ctx = l < DEPTH - 1
        h = _ffn_half(h, mods, ffn1_norm, ffn1_gu, ffn1_dn, l, 0, ns)
        ya, qkv, z, cq, ckv, gate, ab, kr = _mixer_in(h, mods, mix_norm, w_in_p, gm_ln, gm_ws_b, gm_bs_t, l)
        aux, aux_t = _delta_aux(ab, dn_a_log, dn_dt_bias, l)
        qkvn = _delta_conv(qkv, dn_conv, l, seq_ctx)
        o_x, o_c = _delta_rule(qkvn, aux, aux_t, batch, seq_ctx)
        o = jnp.concatenate([o_x, o_c.reshape(1, seq, DN_WIDTH)], 0)
        qh, kh, vh = _mla_prep(cq, ckv, kr, rope, mla_q_norm, w_uq_p, mla_kv_norm, w_uk_p, w_uv, gq, gk, l)
        yc = _attn_latent(qh, kh, vh, batch, seq_ctx)
        n_out = ns if need_ctx else batch
        if need_ctx:
            yc = _attn_context(qh, kh, vh, yc, batch, seq_ctx)
        h = _merge(h, mods, ya, o, z, yc, gate, dn_out_norm, w_a, w_b, w_c, w_o, l, n_out)
        h = _ffn_half(h, mods, ffn2_norm, ffn2_gu, ffn2_dn, l, 6, n_out)
    return h[:batch]
```

```python
import functools
import math

import jax
import jax.numpy as jnp
from jax import lax
from jax.experimental import pallas as pl
from jax.experimental.pallas import tpu as pltpu

F32 = jnp.float32
BF16 = jnp.bfloat16

D_MODEL = 1024
DEPTH = 4
GRID_W = 64
CTX_LEN = 256
RMS_EPS = 1e-6
N_MOD = 9
FFN_HIDDEN = 2816

GM_GROUPS = 4
GM_WIDTH = 512
GM_CHUNK = 128

DN_HEADS = 4
DN_HEAD_DIM = 128
DN_WIDTH = DN_HEADS * DN_HEAD_DIM
DN_CONV = 5
DN_BLOCK = 128

MLA_HEADS = 8
MLA_NOPE = 64
MLA_ROPE = 32
MLA_QK = MLA_NOPE + MLA_ROPE
MLA_V = 64
MLA_WIDTH = MLA_HEADS * MLA_V
MLA_Q_LORA = 384
MLA_KV_LORA = 256
ROPE_BASE = 10000.0
N_BRANCH = 3

LANE = 128
HEAD_PAD = 128
MOD_ROWS = 16
AUX_W = 128
VMEM_LIMIT = 56 * 1024 * 1024

_LOG2E = 1.4426950408889634
FFN_CHUNK = 256
ROW_TILE = 512
MIX_TILE = 256


def _cparams(n_axes):
    return pltpu.CompilerParams(dimension_semantics=("arbitrary",) * n_axes,
                                vmem_limit_bytes=VMEM_LIMIT)


def _dot(a, b):
    return jnp.dot(a, b, preferred_element_type=F32)


def _dot_nt(a, b):
    return lax.dot_general(a, b, (((1,), (1,)), ((), ())), preferred_element_type=F32)


def _sigmoid(x):
    return 1.0 / (1.0 + jnp.exp(-x))


def _silu(x):
    return x * _sigmoid(x)


def _gelu_tanh(x):
    return 0.5 * x * (1.0 + jnp.tanh(math.sqrt(2.0 / math.pi) * (x + 0.044715 * (x * x * x))))


def _const_spec(shape, single=True):
    nd = len(shape)
    kw = {"pipeline_mode": pl.Buffered(1)} if single else {}
    return pl.BlockSpec(shape, lambda *_: (0,) * nd, **kw)


def _layer_spec(layer, shape):
    nd = len(shape)
    return pl.BlockSpec((None,) + tuple(shape), lambda *_: (layer,) + (0,) * nd,
                        pipeline_mode=pl.Buffered(1))


def _mod_spec(layer):
    return pl.BlockSpec((None, None, N_MOD, D_MODEL), lambda s, *_: (layer, s, 0, 0))


def _ada_kernel(x_ref, w_ref, b_ref, o_ref):
    x = x_ref[...]
    xs = _silu(x).astype(BF16)
    o_ref[...] = _dot(xs, w_ref[...].astype(BF16)) + b_ref[...]


def _ada_table(cond, ada_w, ada_b):
    nb = D_MODEL
    out = pl.pallas_call(
        _ada_kernel,
        grid=(DEPTH, N_MOD),
        in_specs=[
            pl.BlockSpec((MOD_ROWS, D_MODEL), lambda l, j: (0, 0)),
            pl.BlockSpec((None, D_MODEL, nb), lambda l, j: (l, 0, j)),
            pl.BlockSpec((None, 1, nb), lambda l, j: (l, 0, j)),
        ],
        out_specs=pl.BlockSpec((None, MOD_ROWS, nb), lambda l, j: (l, 0, j)),
        out_shape=jax.ShapeDtypeStruct((DEPTH, MOD_ROWS, N_MOD * D_MODEL), F32),
        compiler_params=_cparams(2),
        name="ada_table",
    )(cond, ada_w, ada_b.reshape(DEPTH, 1, N_MOD * D_MODEL))
    return out.reshape(DEPTH, MOD_ROWS, N_MOD, D_MODEL)


def _mod_rmsnorm(h, norm_w, shift, scale):
    ms = jnp.mean(h * h, axis=-1, keepdims=True)
    xn = h * lax.rsqrt(ms + RMS_EPS) * norm_w
    return xn * (1.0 + scale) + shift


def _ffn_kernel(h_ref, mod_ref, nw_ref, wgu_ref, wd_ref, o_ref, *, k0):
    h = h_ref[...]
    xm = _mod_rmsnorm(h, nw_ref[...], mod_ref[k0:k0 + 1, :], mod_ref[k0 + 1:k0 + 2, :]).astype(BF16)
    acc = jnp.zeros(h.shape, F32)
    for c in range(FFN_HIDDEN // FFN_CHUNK):
        lo = c * FFN_CHUNK
        g = _dot(xm, wgu_ref[:, lo:lo + FFN_CHUNK])
        u = _dot(xm, wgu_ref[:, FFN_HIDDEN + lo:FFN_HIDDEN + lo + FFN_CHUNK])
        a = (_silu(g) * u).astype(BF16)
        acc = acc + _dot(a, wd_ref[lo:lo + FFN_CHUNK, :])
    o_ref[...] = h + 0.5 * mod_ref[k0 + 2:k0 + 3, :] * acc


def _ffn_half(h, mods, norm_w, w_gu, w_down, layer, k0, n_samples):
    ns, seq, d = h.shape
    tm = min(ROW_TILE, seq)
    tok = pl.BlockSpec((None, tm, d), lambda s, t: (s, t, 0))
    return pl.pallas_call(
        functools.partial(_ffn_kernel, k0=k0),
        grid=(n_samples, seq // tm),
        in_specs=[tok, _mod_spec(layer), _layer_spec(layer, (1, d)),
                  _layer_spec(layer, (d, 2 * FFN_HIDDEN)), _layer_spec(layer, (FFN_HIDDEN, d))],
        out_specs=tok,
        out_shape=jax.ShapeDtypeStruct((n_samples, seq, d), F32),
        compiler_params=_cparams(2),
        name=f"ffn_half_k{k0}",
    )(h, mods, norm_w.reshape(DEPTH, 1, d), w_gu, w_down)


_C_GM = 0
_C_QKV = _C_GM + 2 * GM_WIDTH
_C_Z = _C_QKV + 3 * DN_WIDTH
_C_CQ = _C_Z + DN_WIDTH
_C_CKV = _C_CQ + MLA_Q_LORA
_C_GATE = _C_CKV + MLA_KV_LORA
_C_AB = _C_GATE + N_BRANCH * D_MODEL
_C_KR = _C_AB + AUX_W
_C_END = _C_KR + HEAD_PAD


def _pack_w_in(w_in):
    splits = (2 * GM_WIDTH, 3 * DN_WIDTH, DN_WIDTH, 4 * DN_HEADS, MLA_Q_LORA, MLA_KV_LORA + MLA_ROPE,
              N_BRANCH * D_MODEL)
    offs = [0]
    for s in splits:
        offs.append(offs[-1] + s)
    gm, qkv, z, ab, cq, ckvr, gate = (w_in[..., offs[i]:offs[i + 1]] for i in range(7))
    ckv, kr = ckvr[..., :MLA_KV_LORA], ckvr[..., MLA_KV_LORA:]
    lead = w_in.shape[:-1]
    ab_p = jnp.concatenate([ab, jnp.zeros(lead + (AUX_W - 4 * DN_HEADS,), w_in.dtype)], -1)
    kr_p = jnp.concatenate([jnp.zeros(lead + (MLA_NOPE,), w_in.dtype), kr,
                            jnp.zeros(lead + (HEAD_PAD - MLA_QK,), w_in.dtype)], -1)
    return jnp.concatenate([gm, qkv, z, cq, ckv, gate, ab_p, kr_p], -1).astype(BF16)


def _mixer_in_kernel(h_ref, mod_ref, nw_ref, w_ref, ln_ref, ws_ref, bs_ref,
                     ya_ref, qkv_ref, z_ref, cq_ref, ckv_ref, gate_ref, ab_ref, kr_ref):
    h = h_ref[...]
    um = _mod_rmsnorm(h, nw_ref[...], mod_ref[3:4, :], mod_ref[4:5, :]).astype(BF16)
    qkv_ref[...] = _dot(um, w_ref[:, _C_QKV:_C_Z])
    z_ref[...] = _dot(um, w_ref[:, _C_Z:_C_CQ])
    cq_ref[...] = _dot(um, w_ref[:, _C_CQ:_C_CKV])
    ckv_ref[...] = _dot(um, w_ref[:, _C_CKV:_C_GATE])
    gate_ref[...] = _dot(um, w_ref[:, _C_GATE:_C_AB])
    ab_ref[...] = _dot(um, w_ref[:, _C_AB:_C_KR])
    kr_ref[...] = _dot(um, w_ref[:, _C_KR:_C_END])
    zz = _gelu_tanh(_dot(um, w_ref[:, _C_GM:_C_QKV]))
    uu = zz[:, :GM_WIDTH]
    v = zz[:, GM_WIDTH:]
    mu = jnp.mean(v, axis=-1, keepdims=True)
    vc = v - mu
    var = jnp.mean(vc * vc, axis=-1, keepdims=True)
    vn = (vc * lax.rsqrt(var + RMS_EPS) * ln_ref[...]).astype(BF16)
    gw = GM_WIDTH // GM_GROUPS
    for c in range(h.shape[0] // GM_CHUNK):
        r = slice(c * GM_CHUNK, (c + 1) * GM_CHUNK)
        for g in range(GM_GROUPS):
            cs = slice(g * gw, (g + 1) * gw)
            s = _dot(ws_ref[g], vn[r, cs]) + bs_ref[:, g:g + 1]
            ya_ref[r, cs] = (uu[r, cs] * s).astype(ya_ref.dtype)


def _mixer_in(h, mods, norm_w, w_packed, gm_ln, gm_ws, gm_bs_t, layer):
    ns, seq, d = h.shape
    tm = min(MIX_TILE, seq)

    def tok(width):
        return pl.BlockSpec((None, tm, width), lambda s, t: (s, t, 0))

    def out(width, dtype=F32):
        return jax.ShapeDtypeStruct((ns, seq, width), dtype)

    return pl.pallas_call(
        _mixer_in_kernel,
        grid=(ns, seq // tm),
        in_specs=[tok(d), _mod_spec(layer), _layer_spec(layer, (1, d)), _layer_spec(layer, (d, _C_END)),
                  _layer_spec(layer, (1, GM_WIDTH)),
                  _layer_spec(layer, (GM_GROUPS, GM_CHUNK, GM_CHUNK)),
                  _layer_spec(layer, (GM_CHUNK, GM_GROUPS))],
        out_specs=[tok(GM_WIDTH), tok(3 * DN_WIDTH), tok(DN_WIDTH), tok(MLA_Q_LORA), tok(MLA_KV_LORA),
                   tok(N_BRANCH * d), tok(AUX_W), tok(HEAD_PAD)],
        out_shape=[out(GM_WIDTH, BF16), out(3 * DN_WIDTH), out(DN_WIDTH), out(MLA_Q_LORA),
                   out(MLA_KV_LORA), out(N_BRANCH * d), out(AUX_W), out(HEAD_PAD)],
        compiler_params=_cparams(2),
        name="mixer_in",
    )(h, mods, norm_w.reshape(DEPTH, 1, d), w_packed, gm_ln.reshape(DEPTH, 1, GM_WIDTH), gm_ws, gm_bs_t)


def _tri(n, upper, inclusive=True):
    r = lax.broadcasted_iota(jnp.int32, (n, n), 0)
    c = lax.broadcasted_iota(jnp.int32, (n, n), 1)
    if upper:
        return (r <= c) if inclusive else (r < c)
    return (r >= c) if inclusive else (r > c)


def _delta_aux_kernel(ab_ref, alog_ref, dtb_ref, a_ref, at_ref):
    nh2 = 2 * DN_HEADS
    ab = ab_ref[...]
    sp = jnp.maximum(ab + dtb_ref[...], 0.0) + jnp.log1p(jnp.exp(-jnp.abs(ab + dtb_ref[...])))
    g = -jnp.exp(alog_ref[...]) * sp
    lane = lax.broadcasted_iota(jnp.int32, (DN_BLOCK, AUX_W), 1)
    tri_lo = _tri(DN_BLOCK, False).astype(F32)
    tri_up = _tri(DN_BLOCK, True).astype(F32)
    ones = jnp.ones((DN_BLOCK, DN_BLOCK), F32)
    beta = _sigmoid(ab)
    for c in range(ab.shape[0] // DN_BLOCK):
        r = slice(c * DN_BLOCK, (c + 1) * DN_BLOCK)
        gb = jnp.where(lane < nh2, g[r], 0.0)
        cf = jnp.dot(tri_lo, gb, preferred_element_type=F32, precision=lax.Precision.HIGHEST)
        cr = jnp.dot(tri_up, gb, preferred_element_type=F32, precision=lax.Precision.HIGHEST)
        tot = jnp.dot(ones, gb, preferred_element_type=F32, precision=lax.Precision.HIGHEST)
        gam = jnp.where(lane < DN_HEADS, cf, cr)
        tot_sh = pltpu.roll(tot, 2 * nh2, 1)
        blk = jnp.where(lane < nh2, gam, jnp.where(lane < 2 * nh2, beta[r], jnp.where(lane < 3 * nh2, tot_sh, 0.0)))
        a_ref[r, :] = blk
        at_ref[c] = blk.T[:4 * nh2, :]


def _delta_aux(ab, a_log, dt_bias, layer):
    ns, seq, _ = ab.shape
    nblk = seq // DN_BLOCK
    nh2 = 2 * DN_HEADS

    def row(p):
        return jnp.concatenate([p.reshape(DEPTH, 1, nh2), jnp.zeros((DEPTH, 1, AUX_W - nh2), F32)], -1)

    return pl.pallas_call(
        _delta_aux_kernel,
        grid=(ns,),
        in_specs=[pl.BlockSpec((None, seq, AUX_W), lambda s: (s, 0, 0)),
                  _layer_spec(layer, (1, AUX_W)), _layer_spec(layer, (1, AUX_W))],
        out_specs=[pl.BlockSpec((None, seq, AUX_W), lambda s: (s, 0, 0)),
                   pl.BlockSpec((None, nblk, 4 * nh2, DN_BLOCK), lambda s: (s, 0, 0, 0))],
        out_shape=[jax.ShapeDtypeStruct((ns, seq, AUX_W), F32),
                   jax.ShapeDtypeStruct((ns, nblk, 4 * nh2, DN_BLOCK), F32)],
        compiler_params=_cparams(1),
        name="delta_aux",
    )(ab, row(a_log), row(dt_bias))


_CONV_SUB = 256
_CONV_HALO = 8


def _delta_conv_kernel(x_ref, w_ref, o_ref, xp_ref, *, seq_ctx):
    s = pl.program_id(0)
    j = pl.program_id(1)
    seq = x_ref.shape[0]
    width = x_ref.shape[1]
    zeros = jnp.zeros((_CONV_HALO, width), F32)
    xp_ref[0:_CONV_HALO, :] = zeros
    xp_ref[_CONV_HALO + seq:, :] = zeros
    xp_ref[_CONV_HALO:_CONV_HALO + seq, :] = x_ref[...]
    is_ctx = s == pl.num_programs(0) - 1
    pad = DN_CONV // 2
    edge = _CONV_HALO
    assert seq_ctx == _CONV_SUB
    row_e = lax.broadcasted_iota(jnp.int32, (edge, 1), 0)

    def taps(start, nrows, ok_fn):
        acc = jnp.zeros((nrows, width), F32)
        for k in range(DN_CONV):
            xs = xp_ref[_CONV_HALO + start + k - pad:_CONV_HALO + start + k - pad + nrows, :]
            if ok_fn is not None:
                xs = jnp.where(ok_fn(k - pad), xs, 0.0)
            acc = acc + xs * w_ref[k:k + 1, :]
        return acc

    not_ctx = jnp.logical_not(is_ctx)
    for t in range(seq // _CONV_SUB):
        r0 = t * _CONV_SUB
        mid = taps(r0, _CONV_SUB, None)
        top = taps(r0, edge, lambda sh: not_ctx | (row_e + sh >= 0))
        bot = taps(r0 + _CONV_SUB - edge, edge, lambda sh: not_ctx | (row_e + sh < edge))
        acc = jnp.concatenate([top, mid[edge:_CONV_SUB - edge], bot], axis=0)
        y = _silu(acc)
        for hd in range(width // DN_HEAD_DIM):
            cs = slice(hd * DN_HEAD_DIM, (hd + 1) * DN_HEAD_DIM)
            yh = y[:, cs]
            inv = lax.rsqrt(jnp.sum(yh * yh, axis=-1, keepdims=True) + RMS_EPS)
            fac = jnp.where(j == 0, inv * DN_HEAD_DIM ** -0.5, jnp.where(j == 1, inv, 1.0))
            o_ref[r0:r0 + _CONV_SUB, cs] = yh * fac


def _delta_conv(qkv, conv_w, layer, seq_ctx):
    ns, seq, _ = qkv.shape
    blk = pl.BlockSpec((None, seq, DN_WIDTH), lambda s, j: (s, 0, j))
    return pl.pallas_call(
        functools.partial(_delta_conv_kernel, seq_ctx=seq_ctx),
        grid=(ns, 3),
        in_specs=[blk, pl.BlockSpec((None, DN_CONV, DN_WIDTH), lambda s, j: (layer, 0, j))],
        out_specs=blk,
        out_shape=jax.ShapeDtypeStruct((ns, seq, 3 * DN_WIDTH), F32),
        scratch_shapes=[pltpu.VMEM((seq + 2 * _CONV_HALO, DN_WIDTH), F32)],
        compiler_params=_cparams(2),
        name="delta_conv",
    )(qkv, conv_w)


def _delta_step(q_ref, k_ref, v_ref, a_ref, at_ref, o_ref, s_ref, i, nblk):
    nh2 = 2 * DN_HEADS
    n = DN_BLOCK
    ids = [(d, hd) for d in range(2) for hd in range(DN_HEADS)]
    bis = [i, nblk - 1 - i]
    rows = [pl.ds(pl.multiple_of(bi * n, n), n) for bi in bis]
    ablk = [a_ref[r, :] for r in rows]
    atblk = [at_ref[bi] for bi in bis]
    cols = [slice(hd * DN_HEAD_DIM, (hd + 1) * DN_HEAD_DIM) for hd in range(DN_HEADS)]

    r_i = lax.broadcasted_iota(jnp.int32, (n, n), 0)
    c_i = lax.broadcasted_iota(jnp.int32, (n, n), 1)
    x_i = r_i ^ c_i
    eye = (r_i == c_i).astype(F32)
    incl = [r_i >= c_i, r_i <= c_i]
    strict = [r_i > c_i, r_i < c_i]

    def col(d, hd, grp):
        c = grp * nh2 + d * DN_HEADS + hd
        return ablk[d][:, c:c + 1]

    def row(d, hd, grp):
        c = grp * nh2 + d * DN_HEADS + hd
        return atblk[d][c:c + 1, :]

    q = [q_ref[rows[d], cols[hd]] for d, hd in ids]
    k = [k_ref[rows[d], cols[hd]] for d, hd in ids]
    v = [v_ref[rows[d], cols[hd]] for d, hd in ids]
    gam_c = [col(d, hd, 0) for d, hd in ids]
    beta_c = [col(d, hd, 1) for d, hd in ids]
    tot_c = [col(d, hd, 2) for d, hd in ids]
    decay = [jnp.where(incl[d], jnp.exp(jnp.where(incl[d], col(d, hd, 0) - row(d, hd, 0), 0.0)), 0.0)
             for d, hd in ids]
    kb = [kj * bj for kj, bj in zip(k, beta_c)]
    kf = [kj.astype(BF16) for kj in k]
    kk = [_dot_nt(kbj.astype(BF16), kfj) for kbj, kfj in zip(kb, kf)]
    qk = [_dot_nt(qj.astype(BF16), kfj) for qj, kfj in zip(q, kf)]
    m = [jnp.where(strict[d], kkj * dj, 0.0) for (d, _), kkj, dj in zip(ids, kk, decay)]
    qkd = [(qkj * dj).astype(BF16) for qkj, dj in zip(qk, decay)]

    t = [eye - jnp.where(x_i < 2, mj, 0.0) for mj in m]
    s = 2
    while s < n:
        sel = (x_i >= s) & (x_i < 2 * s)
        tb = [tj.astype(BF16) for tj in t]
        y = [_dot(jnp.where(sel, mj, 0.0).astype(BF16), tbj).astype(BF16) for mj, tbj in zip(m, tb)]
        z = [_dot(tbj, yj) for tbj, yj in zip(tb, y)]
        t = [tj - zj for tj, zj in zip(t, z)]
        s *= 2

    rhs = [jnp.concatenate([vj * bj, kbj * jnp.exp(gj)], axis=1).astype(BF16)
           for vj, bj, kbj, gj in zip(v, beta_c, kb, gam_c)]
    uw = [_dot(tj.astype(BF16), rj) for tj, rj in zip(t, rhs)]
    kd_t = [(kj * jnp.exp(tc - gj)).T.astype(BF16) for kj, tc, gj in zip(k, tot_c, gam_c)]

    s_prev = [s_ref[d * DN_HEADS + hd] for d, hd in ids]
    lhs = [jnp.concatenate([uwj[:, DN_HEAD_DIM:], qj * jnp.exp(gj)], axis=0).astype(BF16)
           for uwj, qj, gj in zip(uw, q, gam_c)]
    ws_qs = [_dot(lj, sj.astype(BF16)) for lj, sj in zip(lhs, s_prev)]
    vnb = [(uwj[:, :DN_HEAD_DIM] - wq[:n]).astype(BF16) for uwj, wq in zip(uw, ws_qs)]
    o = [wq[n:] + _dot(qkj, vj) for wq, qkj, vj in zip(ws_qs, qkd, vnb)]
    s_new = [sj * jnp.exp(row(d, hd, 2)) + _dot(kdj, vj)
             for (d, hd), sj, kdj, vj in zip(ids, s_prev, kd_t, vnb)]
    for (d, hd), oj, sj in zip(ids, o, s_new):
        s_ref[d * DN_HEADS + hd] = sj
        o_ref[rows[d], cols[hd]] += oj


def _delta_kernel(qx_ref, kx_ref, vx_ref, ax_ref, atx_ref, qc_ref, kc_ref, vc_ref, ac_ref, atc_ref,
                  ox_ref, oc_ref, s_ref):
    s_ref[...] = jnp.zeros(s_ref.shape, F32)
    ox_ref[...] = jnp.zeros(ox_ref.shape, F32)
    oc_ref[...] = jnp.zeros(oc_ref.shape, F32)

    def scan(q_ref, k_ref, v_ref, a_ref, at_ref, o_ref):
        nblk = o_ref.shape[0] // DN_BLOCK

        def body(i, carry):
            _delta_step(q_ref, k_ref, v_ref, a_ref, at_ref, o_ref, s_ref, i, nblk)
            return carry
        lax.fori_loop(0, nblk, body, 0)

    scan(qc_ref, kc_ref, vc_ref, ac_ref, atc_ref, oc_ref)
    scan(qx_ref, kx_ref, vx_ref, ax_ref, atx_ref, ox_ref)


def _delta_rule(qkvn, aux, aux_t, batch, seq_ctx):
    ns, seq, _ = qkvn.shape
    nblk = seq // DN_BLOCK
    nh2 = 2 * DN_HEADS

    def lat(j):
        return pl.BlockSpec((None, seq, DN_WIDTH), lambda b: (b, 0, j))

    def ctx(j):
        return pl.BlockSpec((None, seq_ctx, DN_WIDTH), lambda b: (batch, b, j))

    return pl.pallas_call(
        _delta_kernel,
        grid=(batch,),
        in_specs=[lat(0), lat(1), lat(2),
                  pl.BlockSpec((None, seq, AUX_W), lambda b: (b, 0, 0)),
                  pl.BlockSpec((None, nblk, 4 * nh2, DN_BLOCK), lambda b: (b, 0, 0, 0)),
                  ctx(0), ctx(1), ctx(2),
                  pl.BlockSpec((None, seq_ctx, AUX_W), lambda b: (batch, b, 0)),
                  pl.BlockSpec((None, seq_ctx // DN_BLOCK, 4 * nh2, DN_BLOCK), lambda b: (batch, b, 0, 0))],
        out_specs=[pl.BlockSpec((None, seq, DN_WIDTH), lambda b: (b, 0, 0)),
                   pl.BlockSpec((None, seq_ctx, DN_WIDTH), lambda b: (b, 0, 0))],
        out_shape=[jax.ShapeDtypeStruct((batch, seq, DN_WIDTH), F32),
                   jax.ShapeDtypeStruct((batch, seq_ctx, DN_WIDTH), F32)],
        scratch_shapes=[pltpu.VMEM((nh2, DN_HEAD_DIM, DN_HEAD_DIM), F32)],
        compiler_params=_cparams(1),
        name="delta_rule",
    )(qkvn, qkvn, qkvn, aux, aux_t, qkvn, qkvn, qkvn, aux, aux_t)


def _rope_tables(seq):
    rows = seq // GRID_W
    row = jnp.repeat(jnp.arange(rows), GRID_W).astype(F32)
    col = jnp.tile(jnp.arange(GRID_W), rows).astype(F32)
    half = MLA_ROPE // 2
    inv_freq = 1.0 / (ROPE_BASE ** (jnp.arange(0, half, 2, dtype=F32) / half))
    ang_r = row[:, None] * inv_freq
    ang_c = col[:, None] * inv_freq
    ang = jnp.concatenate([ang_r, ang_r, ang_c, ang_c], axis=-1)
    cos, sin = jnp.cos(ang), jnp.sin(ang)
    q4 = half // 2
    z = jnp.zeros((seq, q4), F32)
    sin_up = jnp.concatenate([-sin[:, 0:q4], z, -sin[:, half:half + q4], z], -1)
    sin_dn = jnp.concatenate([z, sin[:, q4:half], z, sin[:, half + q4:]], -1)

    def pad(t, fill):
        return jnp.concatenate([jnp.full((seq, MLA_NOPE), fill, F32), t,
                                jnp.full((seq, HEAD_PAD - MLA_QK), fill, F32)], -1)

    lat = jnp.stack([pad(cos, 1.0), pad(sin_up, 0.0), pad(sin_dn, 0.0)])
    ident = jnp.stack([jnp.ones((seq, HEAD_PAD), F32), jnp.zeros((seq, HEAD_PAD), F32),
                       jnp.zeros((seq, HEAD_PAD), F32)])
    return jnp.stack([lat, ident])


def _head_norm_rope(xh, gain, cos, sin_up, sin_dn):
    ms = jnp.sum(xh * xh, axis=-1, keepdims=True) * (1.0 / MLA_QK)
    y = xh * lax.rsqrt(ms + RMS_EPS) * gain
    q4 = MLA_ROPE // 4
    return y * cos + pltpu.roll(y, HEAD_PAD - q4, 1) * sin_up + pltpu.roll(y, q4, 1) * sin_dn


def _mla_prep_kernel(cq_ref, ckv_ref, kr_ref, rope_ref, qn_ref, wuq_ref, kvn_ref, wuk_ref, wuv_ref,
                     gq_ref, gk_ref, q_ref, k_ref, v_ref):
    cos, sin_up, sin_dn = rope_ref[0], rope_ref[1], rope_ref[2]
    cq = cq_ref[...]
    cqn = (cq * lax.rsqrt(jnp.mean(cq * cq, axis=-1, keepdims=True) + RMS_EPS) * qn_ref[...]).astype(BF16)
    q = _dot(cqn, wuq_ref[...])
    ckv = ckv_ref[...]
    ckvn = (ckv * lax.rsqrt(jnp.mean(ckv * ckv, axis=-1, keepdims=True) + RMS_EPS) * kvn_ref[...]).astype(BF16)
    kn = _dot(ckvn, wuk_ref[...])
    lane = lax.broadcasted_iota(jnp.int32, (1, MLA_HEADS * HEAD_PAD), 1)
    ones_pad = jnp.where(lane % HEAD_PAD >= MLA_V, 1.0, 0.0)
    v_ref[...] = (_dot(ckvn, wuv_ref[...]) + ones_pad).astype(v_ref.dtype)
    kr = kr_ref[...]
    for hd in range(MLA_HEADS):
        cs = slice(hd * HEAD_PAD, (hd + 1) * HEAD_PAD)
        qh = _head_norm_rope(q[:, cs], gq_ref[...], cos, sin_up, sin_dn)
        q_ref[:, cs] = (qh * (MLA_QK ** -0.5 * _LOG2E)).astype(q_ref.dtype)
        kh = _head_norm_rope(kn[:, cs] + kr, gk_ref[...], cos, sin_up, sin_dn)
        k_ref[:, cs] = kh.astype(k_ref.dtype)


def _pad_heads(w, valid, lo=0):
    lead = w.shape[:-1]
    w = w.reshape(lead + (MLA_HEADS, valid))
    parts = []
    if lo:
        parts.append(jnp.zeros(lead + (MLA_HEADS, lo), w.dtype))
    parts.append(w)
    parts.append(jnp.zeros(lead + (MLA_HEADS, HEAD_PAD - lo - valid), w.dtype))
    return jnp.concatenate(parts, -1).reshape(lead + (MLA_HEADS * HEAD_PAD,))


def _pad_gain(g):
    return jnp.concatenate([g, jnp.zeros(g.shape[:-1] + (HEAD_PAD - MLA_QK,), g.dtype)], -1)[:, None, :]


def _mla_prep(cq, ckv, kr, rope, q_norm, w_uq_p, kv_norm, w_uk_p, w_uv, gq, gk, layer):
    ns, seq, _ = cq.shape
    tm = min(ROW_TILE, seq)

    def tok(width):
        return pl.BlockSpec((None, tm, width), lambda s, t: (s, t, 0))

    wq = MLA_HEADS * HEAD_PAD
    return pl.pallas_call(
        _mla_prep_kernel,
        grid=(ns, seq // tm),
        in_specs=[tok(MLA_Q_LORA), tok(MLA_KV_LORA), tok(HEAD_PAD),
                  pl.BlockSpec((None, 3, tm, HEAD_PAD), lambda s, t: (s // (ns - 1), 0, t, 0)),
                  _layer_spec(layer, (1, MLA_Q_LORA)), _layer_spec(layer, (MLA_Q_LORA, wq)),
                  _layer_spec(layer, (1, MLA_KV_LORA)), _layer_spec(layer, (MLA_KV_LORA, wq)),
                  _layer_spec(layer, (MLA_KV_LORA, wq)),
                  _layer_spec(layer, (1, HEAD_PAD)), _layer_spec(layer, (1, HEAD_PAD))],
        out_specs=[tok(wq), tok(wq), tok(wq)],
        out_shape=[jax.ShapeDtypeStruct((ns, seq, wq), BF16), jax.ShapeDtypeStruct((ns, seq, wq), BF16),
                   jax.ShapeDtypeStruct((ns, seq, wq), BF16)],
        compiler_params=_cparams(2),
        name="mla_prep",
    )(cq, ckv, kr, rope, q_norm.reshape(DEPTH, 1, MLA_Q_LORA), w_uq_p, kv_norm.reshape(DEPTH, 1, MLA_KV_LORA),
      w_uk_p, w_uv, gq, gk)


def _attn_kernel(*refs, n_src):
    q_ref = refs[0]
    k_refs = refs[1:1 + n_src]
    v_refs = refs[1 + n_src:1 + 2 * n_src]
    o_ref = refs[1 + 2 * n_src]
    n_heads = q_ref.shape[1] // HEAD_PAD

    def cols(hh):
        return slice(hh * HEAD_PAD, (hh + 1) * HEAD_PAD)

    def score(hh):
        return [_dot_nt(q_ref[:, cols(hh)], k_ref[:, cols(hh)]) for k_ref in k_refs]

    def probs(scores):
        m = scores[0].max(axis=-1, keepdims=True)
        for sc in scores[1:]:
            m = jnp.maximum(m, sc.max(axis=-1, keepdims=True))
        return [jnp.exp2(sc - m).astype(BF16) for sc in scores]

    def weighted(hh, ps):
        acc = _dot(ps[0], v_refs[0][:, cols(hh)])
        for p, v_ref in zip(ps[1:], v_refs[1:]):
            acc = acc + _dot(p, v_ref[:, cols(hh)])
        return acc / acc[:, MLA_V:MLA_V + 1]

    ahead = 2
    scores = {hh: score(hh) for hh in range(min(ahead, n_heads))}
    heads = []
    for hh in range(n_heads):
        ps = probs(scores.pop(hh))
        if hh + ahead < n_heads:
            scores[hh + ahead] = score(hh + ahead)
        heads.append(weighted(hh, ps))
    lane = lax.broadcasted_iota(jnp.int32, (q_ref.shape[0], HEAD_PAD), 1)
    for pr in range(n_heads // 2):
        out = jnp.where(lane < MLA_V, heads[2 * pr], pltpu.roll(heads[2 * pr + 1], MLA_V, 1))
        o_ref[:, pr * 2 * MLA_V:(pr + 1) * 2 * MLA_V] = out.astype(o_ref.dtype)


ATTN_Q_TILE = 512
ATTN_HEADS_PER_STEP = 4


def _attn_latent(q, k, v, batch, seq_ctx):
    ns, seq, _ = q.shape
    tq = min(ATTN_Q_TILE, seq)
    pair = ATTN_HEADS_PER_STEP * HEAD_PAD
    return pl.pallas_call(
        functools.partial(_attn_kernel, n_src=2),
        grid=(batch, MLA_HEADS // ATTN_HEADS_PER_STEP, seq // tq),
        in_specs=[pl.BlockSpec((None, tq, pair), lambda b, h, t: (b, t, h)),
                  pl.BlockSpec((None, seq, pair), lambda b, h, t: (b, 0, h)),
                  pl.BlockSpec((None, seq_ctx, pair), lambda b, h, t: (batch, b, h)),
                  pl.BlockSpec((None, seq, pair), lambda b, h, t: (b, 0, h)),
                  pl.BlockSpec((None, seq_ctx, pair), lambda b, h, t: (batch, b, h))],
        out_specs=pl.BlockSpec((None, tq, ATTN_HEADS_PER_STEP * MLA_V), lambda b, h, t: (b, t, h)),
        out_shape=jax.ShapeDtypeStruct((ns, seq, MLA_WIDTH), BF16),
        compiler_params=_cparams(3),
        name="attn_latent",
    )(q, k, k, v, v)


def _attn_context(q, k, v, yc, batch, seq_ctx):
    ns, seq, _ = q.shape
    pair = ATTN_HEADS_PER_STEP * HEAD_PAD

    def kern(q_ref, k_ref, v_ref, yc_in_ref, o_ref):
        del yc_in_ref
        _attn_kernel(q_ref, k_ref, v_ref, o_ref, n_src=1)

    return pl.pallas_call(
        kern,
        grid=(batch, MLA_HEADS // ATTN_HEADS_PER_STEP),
        in_specs=[pl.BlockSpec((None, seq_ctx, pair), lambda b, h: (batch, b, h)),
                  pl.BlockSpec((None, seq_ctx, pair), lambda b, h: (batch, b, h)),
                  pl.BlockSpec((None, seq_ctx, pair), lambda b, h: (batch, b, h)),
                  pl.BlockSpec(memory_space=pl.ANY)],
        out_specs=pl.BlockSpec((None, seq_ctx, ATTN_HEADS_PER_STEP * MLA_V), lambda b, h: (batch, b, h)),
        out_shape=jax.ShapeDtypeStruct(yc.shape, yc.dtype),
        input_output_aliases={3: 0},
        compiler_params=_cparams(2),
        name="attn_context",
    )(q, k, v, yc)


def _merge_kernel(h_ref, mod_ref, ya_ref, o_ref, z_ref, yc_ref, gate_ref, onw_ref,
                  wa_ref, wb_ref, wc_ref, wo_ref, out_ref):
    d = h_ref.shape[1]
    o = o_ref[...]
    z = z_ref[...]
    yb_parts = []
    for hd in range(DN_HEADS):
        cs = slice(hd * DN_HEAD_DIM, (hd + 1) * DN_HEAD_DIM)
        oh = o[:, cs]
        ohn = oh * lax.rsqrt(jnp.mean(oh * oh, axis=-1, keepdims=True) + RMS_EPS) * onw_ref[...]
        yb_parts.append(ohn * _silu(z[:, cs]))
    yb = jnp.concatenate(yb_parts, axis=1).astype(BF16)
    m = (_sigmoid(gate_ref[:, 0:d]) * _dot(ya_ref[...], wa_ref[...])
         + _sigmoid(gate_ref[:, d:2 * d]) * _dot(yb, wb_ref[...])
         + _sigmoid(gate_ref[:, 2 * d:3 * d]) * _dot(yc_ref[...], wc_ref[...]))
    y = _dot(m.astype(BF16), wo_ref[...])
    out_ref[...] = h_ref[...] + mod_ref[5:6, :] * y


def _merge(h, mods, ya, o, z, yc, gate, out_norm, w_a, w_b, w_c, w_o, layer, n_samples):
    ns, seq, d = h.shape
    tm = min(ROW_TILE, seq)

    def tok(width):
        return pl.BlockSpec((None, tm, width), lambda s, t: (s, t, 0))

    return pl.pallas_call(
        _merge_kernel,
        grid=(n_samples, seq // tm),
        in_specs=[tok(d), _mod_spec(layer), tok(GM_WIDTH), tok(DN_WIDTH), tok(DN_WIDTH), tok(MLA_WIDTH),
                  tok(N_BRANCH * d), _layer_spec(layer, (1, DN_HEAD_DIM)),
                  _layer_spec(layer, (GM_WIDTH, d)), _layer_spec(layer, (DN_WIDTH, d)),
                  _layer_spec(layer, (MLA_WIDTH, d)), _layer_spec(layer, (d, d))],
        out_specs=tok(d),
        out_shape=jax.ShapeDtypeStruct((n_samples, seq, d), F32),
        compiler_params=_cparams(2),
        name="merge",
    )(h, mods, ya, o, z, yc, gate, out_norm.reshape(DEPTH, 1, DN_HEAD_DIM), w_a, w_b, w_c, w_o)


def kernel(x, c, ctx, c_ctx, ada_w, ada_b, ffn1_norm, ffn1_w_gu, ffn1_w_down, mix_norm, w_in, gm_ln, gm_ws, gm_bs, dn_conv, dn_a_log, dn_dt_bias, dn_out_norm, mla_q_norm, mla_w_uq, mla_kv_norm, mla_w_ukv, mla_qk_norm_q, mla_qk_norm_k, w_branch_gm, w_branch_dn, w_branch_mla, w_out, ffn2_norm, ffn2_w_gu, ffn2_w_down):
    batch, seq, d = x.shape
    seq_ctx = ctx.shape[1]
    assert d == D_MODEL and batch * seq_ctx == seq and batch + 1 <= MOD_ROWS
    assert seq % ROW_TILE == 0 or seq < ROW_TILE
    ns = batch + 1

    cond = jnp.concatenate([c, c_ctx[None, :], jnp.zeros((MOD_ROWS - ns, d), F32)], 0)
    mods = _ada_table(cond, ada_w, ada_b)
    h = jnp.concatenate([x, ctx.reshape(1, seq, d)], 0)

    bf = lambda w: w.astype(BF16)
    ffn1_gu, ffn1_dn, ffn2_gu, ffn2_dn = bf(ffn1_w_gu), bf(ffn1_w_down), bf(ffn2_w_gu), bf(ffn2_w_down)
    w_in_p = _pack_w_in(w_in)
    gm_ws_b = bf(gm_ws)
    gm_bs_t = jnp.swapaxes(gm_bs, 1, 2)
    w_uq_p = bf(_pad_heads(mla_w_uq, MLA_QK))
    ukv = mla_w_ukv.reshape(DEPTH, MLA_KV_LORA, MLA_HEADS, MLA_NOPE + MLA_V)
    w_uk_p = bf(_pad_heads(ukv[..., :MLA_NOPE].reshape(DEPTH, MLA_KV_LORA, MLA_HEADS * MLA_NOPE), MLA_NOPE))
    w_uv = bf(_pad_heads(ukv[..., MLA_NOPE:].reshape(DEPTH, MLA_KV_LORA, MLA_WIDTH), MLA_V))
    gq, gk = _pad_gain(mla_qk_norm_q), _pad_gain(mla_qk_norm_k)
    w_a, w_b, w_c, w_o = bf(w_branch_gm), bf(w_branch_dn), bf(w_branch_mla), bf(w_out)
    rope = _rope_tables(seq)

    for l in range(DEPTH):
        need_ctx = l < DEPTH - 1
        h = _ffn_half(h, mods, ffn1_norm, ffn1_gu, ffn1_dn, l, 0, ns)
        ya, qkv, z, cq, ckv, gate, ab, kr = _mixer_in(h, mods, mix_norm, w_in_p, gm_ln, gm_ws_b, gm_bs_t, l)
        aux, aux_t = _delta_aux(ab, dn_a_log, dn_dt_bias, l)
        qkvn = _delta_conv(qkv, dn_conv, l, seq_ctx)
        o_x, o_c = _delta_rule(qkvn, aux, aux_t, batch, seq_ctx)
        o = jnp.concatenate([o_x, o_c.reshape(1, seq, DN_WIDTH)], 0)
        qh, kh, vh = _mla_prep(cq, ckv, kr, rope, mla_q_norm, w_uq_p, mla_kv_norm, w_uk_p, w_uv, gq, gk, l)
        yc = _attn_latent(qh, kh, vh, batch, seq_ctx)
        n_out = ns if need_ctx else batch
        if need_ctx:
            yc = _attn_context(qh, kh, vh, yc, batch, seq_ctx)
        h = _merge(h, mods, ya, o, z, yc, gate, dn_out_norm, w_a, w_b, w_c, w_o, l, n_out)
        h = _ffn_half(h, mods, ffn2_norm, ffn2_gu, ffn2_dn, l, 6, n_out)
    return h[:batch]
```

```python
import functools
import math

import jax
import jax.numpy as jnp
from jax import lax
from jax.experimental import pallas as pl
from jax.experimental.pallas import tpu as pltpu

F32 = jnp.float32
BF16 = jnp.bfloat16

D_MODEL = 1024
DEPTH = 4
GRID_W = 64
CTX_LEN = 256
RMS_EPS = 1e-6
N_MOD = 9
FFN_HIDDEN = 2816

GM_GROUPS = 4
GM_WIDTH = 512
GM_CHUNK = 128

DN_HEADS = 4
DN_HEAD_DIM = 128
DN_WIDTH = DN_HEADS * DN_HEAD_DIM
DN_CONV = 5
DN_BLOCK = 128

MLA_HEADS = 8
MLA_NOPE = 64
MLA_ROPE = 32
MLA_QK = MLA_NOPE + MLA_ROPE
MLA_V = 64
MLA_WIDTH = MLA_HEADS * MLA_V
MLA_Q_LORA = 384
MLA_KV_LORA = 256
ROPE_BASE = 10000.0
N_BRANCH = 3

LANE = 128
HEAD_PAD = 128
MOD_ROWS = 16
AUX_W = 128
VMEM_LIMIT = 56 * 1024 * 1024

_LOG2E = 1.4426950408889634
FFN_CHUNK = 256
ROW_TILE = 512
MIX_TILE = 512


def _cparams(n_axes):
    return pltpu.CompilerParams(dimension_semantics=("arbitrary",) * n_axes,
                                vmem_limit_bytes=VMEM_LIMIT)


def _dot(a, b):
    return jnp.dot(a, b, preferred_element_type=F32)


def _dot_nt(a, b):
    return lax.dot_general(a, b, (((1,), (1,)), ((), ())), preferred_element_type=F32)


def _sigmoid(x):
    return 1.0 / (1.0 + jnp.exp(-x))


def _silu(x):
    return x * _sigmoid(x)


def _gelu_tanh(x):
    return 0.5 * x * (1.0 + jnp.tanh(math.sqrt(2.0 / math.pi) * (x + 0.044715 * (x * x * x))))


def _const_spec(shape, single=True):
    nd = len(shape)
    kw = {"pipeline_mode": pl.Buffered(1)} if single else {}
    return pl.BlockSpec(shape, lambda *_: (0,) * nd, **kw)


def _layer_spec(layer, shape):
    nd = len(shape)
    return pl.BlockSpec((None,) + tuple(shape), lambda *_: (layer,) + (0,) * nd,
                        pipeline_mode=pl.Buffered(1))


def _mod_spec(layer):
    return pl.BlockSpec((None, None, N_MOD, D_MODEL), lambda s, *_: (layer, s, 0, 0))


def _ada_kernel(x_ref, w_ref, b_ref, o_ref):
    x = x_ref[...]
    xs = _silu(x).astype(BF16)
    o_ref[...] = _dot(xs, w_ref[...].astype(BF16)) + b_ref[...]


def _ada_table(cond, ada_w, ada_b):
    nb = D_MODEL
    out = pl.pallas_call(
        _ada_kernel,
        grid=(DEPTH, N_MOD),
        in_specs=[
            pl.BlockSpec((MOD_ROWS, D_MODEL), lambda l, j: (0, 0)),
            pl.BlockSpec((None, D_MODEL, nb), lambda l, j: (l, 0, j)),
            pl.BlockSpec((None, 1, nb), lambda l, j: (l, 0, j)),
        ],
        out_specs=pl.BlockSpec((None, MOD_ROWS, nb), lambda l, j: (l, 0, j)),
        out_shape=jax.ShapeDtypeStruct((DEPTH, MOD_ROWS, N_MOD * D_MODEL), F32),
        compiler_params=_cparams(2),
        name="ada_table",
    )(cond, ada_w, ada_b.reshape(DEPTH, 1, N_MOD * D_MODEL))
    return out.reshape(DEPTH, MOD_ROWS, N_MOD, D_MODEL)


def _mod_rmsnorm(h, norm_w, shift, scale):
    ms = jnp.mean(h * h, axis=-1, keepdims=True)
    xn = h * lax.rsqrt(ms + RMS_EPS) * norm_w
    return xn * (1.0 + scale) + shift


def _ffn_kernel(h_ref, mod_ref, nw_ref, wgu_ref, wd_ref, o_ref, *, k0):
    h = h_ref[...]
    xm = _mod_rmsnorm(h, nw_ref[...], mod_ref[k0:k0 + 1, :], mod_ref[k0 + 1:k0 + 2, :]).astype(BF16)
    acc = jnp.zeros(h.shape, F32)
    for c in range(FFN_HIDDEN // FFN_CHUNK):
        lo = c * FFN_CHUNK
        g = _dot(xm, wgu_ref[:, lo:lo + FFN_CHUNK])
        u = _dot(xm, wgu_ref[:, FFN_HIDDEN + lo:FFN_HIDDEN + lo + FFN_CHUNK])
        a = (_silu(g) * u).astype(BF16)
        acc = acc + _dot(a, wd_ref[lo:lo + FFN_CHUNK, :])
    o_ref[...] = h + 0.5 * mod_ref[k0 + 2:k0 + 3, :] * acc


def _ffn_half(h, mods, norm_w, w_gu, w_down, layer, k0, n_samples):
    ns, seq, d = h.shape
    tm = min(ROW_TILE, seq)
    tok = pl.BlockSpec((None, tm, d), lambda s, t: (s, t, 0))
    return pl.pallas_call(
        functools.partial(_ffn_kernel, k0=k0),
        grid=(n_samples, seq // tm),
        in_specs=[tok, _mod_spec(layer), _layer_spec(layer, (1, d)),
                  _layer_spec(layer, (d, 2 * FFN_HIDDEN)), _layer_spec(layer, (FFN_HIDDEN, d))],
        out_specs=tok,
        out_shape=jax.ShapeDtypeStruct((n_samples, seq, d), F32),
        compiler_params=_cparams(2),
        name=f"ffn_half_k{k0}",
    )(h, mods, norm_w.reshape(DEPTH, 1, d), w_gu, w_down)


_C_GM = 0
_C_QKV = _C_GM + 2 * GM_WIDTH
_C_Z = _C_QKV + 3 * DN_WIDTH
_C_CQ = _C_Z + DN_WIDTH
_C_CKV = _C_CQ + MLA_Q_LORA
_C_GATE = _C_CKV + MLA_KV_LORA
_C_AB = _C_GATE + N_BRANCH * D_MODEL
_C_KR = _C_AB + AUX_W
_C_KRR = _C_KR + HEAD_PAD
_C_END = _C_KRR + HEAD_PAD


def _rot_half_cols(w):
    q4 = MLA_ROPE // 4
    a, b, c, e = (w[..., i * q4:(i + 1) * q4] for i in range(4))
    return jnp.concatenate([-b, a, -e, c], -1)


def _rot_half_gain(g):
    q4 = MLA_ROPE // 4
    a, b, c, e = (g[..., i * q4:(i + 1) * q4] for i in range(4))
    return jnp.concatenate([b, a, e, c], -1)


def _pack_w_in(w_in):
    splits = (2 * GM_WIDTH, 3 * DN_WIDTH, DN_WIDTH, 4 * DN_HEADS, MLA_Q_LORA, MLA_KV_LORA + MLA_ROPE,
              N_BRANCH * D_MODEL)
    offs = [0]
    for s in splits:
        offs.append(offs[-1] + s)
    gm, qkv, z, ab, cq, ckvr, gate = (w_in[..., offs[i]:offs[i + 1]] for i in range(7))
    ckv, kr = ckvr[..., :MLA_KV_LORA], ckvr[..., MLA_KV_LORA:]
    lead = w_in.shape[:-1]
    ab_p = jnp.concatenate([ab, jnp.zeros(lead + (AUX_W - 4 * DN_HEADS,), w_in.dtype)], -1)
    def rope_group(cols):
        return jnp.concatenate([jnp.zeros(lead + (MLA_NOPE,), w_in.dtype), cols,
                                jnp.zeros(lead + (HEAD_PAD - MLA_QK,), w_in.dtype)], -1)

    return jnp.concatenate([gm, qkv, z, cq, ckv, gate, ab_p, rope_group(kr), rope_group(_rot_half_cols(kr))],
                           -1).astype(BF16)


def _mixer_in_kernel(h_ref, mod_ref, nw_ref, w_ref, ln_ref, ws_ref, bs_ref,
                     ya_ref, qkv_ref, z_ref, cq_ref, ckv_ref, gate_ref, ab_ref, kr_ref, krr_ref):
    h = h_ref[...]
    um = _mod_rmsnorm(h, nw_ref[...], mod_ref[3:4, :], mod_ref[4:5, :]).astype(BF16)
    qkv_ref[...] = _dot(um, w_ref[:, _C_QKV:_C_Z]).astype(qkv_ref.dtype)
    z_ref[...] = _dot(um, w_ref[:, _C_Z:_C_CQ]).astype(z_ref.dtype)
    cq_ref[...] = _dot(um, w_ref[:, _C_CQ:_C_CKV])
    ckv_ref[...] = _dot(um, w_ref[:, _C_CKV:_C_GATE])
    gate_ref[...] = _dot(um, w_ref[:, _C_GATE:_C_AB]).astype(gate_ref.dtype)
    ab_ref[...] = _dot(um, w_ref[:, _C_AB:_C_KR])
    kr_ref[...] = _dot(um, w_ref[:, _C_KR:_C_KRR])
    krr_ref[...] = _dot(um, w_ref[:, _C_KRR:_C_END])
    zz = _gelu_tanh(_dot(um, w_ref[:, _C_GM:_C_QKV]))
    uu = zz[:, :GM_WIDTH]
    v = zz[:, GM_WIDTH:]
    mu = jnp.mean(v, axis=-1, keepdims=True)
    vc = v - mu
    var = jnp.mean(vc * vc, axis=-1, keepdims=True)
    vn = (vc * lax.rsqrt(var + RMS_EPS) * ln_ref[...]).astype(BF16)
    gw = GM_WIDTH // GM_GROUPS
    for c in range(h.shape[0] // GM_CHUNK):
        r = slice(c * GM_CHUNK, (c + 1) * GM_CHUNK)
        for g in range(GM_GROUPS):
            cs = slice(g * gw, (g + 1) * gw)
            s = _dot(ws_ref[g], vn[r, cs]) + bs_ref[:, g:g + 1]
            ya_ref[r, cs] = (uu[r, cs] * s).astype(ya_ref.dtype)


def _mixer_in(h, mods, norm_w, w_packed, gm_ln, gm_ws, gm_bs_t, layer):
    ns, seq, d = h.shape
    tm = min(MIX_TILE, seq)

    def tok(width):
        return pl.BlockSpec((None, tm, width), lambda s, t: (s, t, 0))

    def out(width, dtype=F32):
        return jax.ShapeDtypeStruct((ns, seq, width), dtype)

    return pl.pallas_call(
        _mixer_in_kernel,
        grid=(ns, seq // tm),
        in_specs=[tok(d), _mod_spec(layer), _layer_spec(layer, (1, d)), _layer_spec(layer, (d, _C_END)),
                  _layer_spec(layer, (1, GM_WIDTH)),
                  _layer_spec(layer, (GM_GROUPS, GM_CHUNK, GM_CHUNK)),
                  _layer_spec(layer, (GM_CHUNK, GM_GROUPS))],
        out_specs=[tok(GM_WIDTH), tok(3 * DN_WIDTH), tok(DN_WIDTH), tok(MLA_Q_LORA), tok(MLA_KV_LORA),
                   tok(N_BRANCH * d), tok(AUX_W), tok(HEAD_PAD), tok(HEAD_PAD)],
        out_shape=[out(GM_WIDTH, BF16), out(3 * DN_WIDTH, BF16), out(DN_WIDTH, BF16), out(MLA_Q_LORA),
                   out(MLA_KV_LORA), out(N_BRANCH * d, BF16), out(AUX_W), out(HEAD_PAD), out(HEAD_PAD)],
        compiler_params=_cparams(2),
        name="mixer_in",
    )(h, mods, norm_w.reshape(DEPTH, 1, d), w_packed, gm_ln.reshape(DEPTH, 1, GM_WIDTH), gm_ws, gm_bs_t)


def _tri(n, upper, inclusive=True):
    r = lax.broadcasted_iota(jnp.int32, (n, n), 0)
    c = lax.broadcasted_iota(jnp.int32, (n, n), 1)
    if upper:
        return (r <= c) if inclusive else (r < c)
    return (r >= c) if inclusive else (r > c)


def _delta_aux_kernel(ab_ref, alog_ref, dtb_ref, a_ref, at_ref):
    nh2 = 2 * DN_HEADS
    ab = ab_ref[...]
    sp = jnp.maximum(ab + dtb_ref[...], 0.0) + jnp.log1p(jnp.exp(-jnp.abs(ab + dtb_ref[...])))
    g = -jnp.exp(alog_ref[...]) * sp
    lane = lax.broadcasted_iota(jnp.int32, (DN_BLOCK, AUX_W), 1)
    tri_lo = _tri(DN_BLOCK, False).astype(F32)
    tri_up = _tri(DN_BLOCK, True).astype(F32)
    ones = jnp.ones((DN_BLOCK, DN_BLOCK), F32)
    beta = _sigmoid(ab)
    for c in range(ab.shape[0] // DN_BLOCK):
        r = slice(c * DN_BLOCK, (c + 1) * DN_BLOCK)
        gb = jnp.where(lane < nh2, g[r], 0.0)
        cf = jnp.dot(tri_lo, gb, preferred_element_type=F32, precision=lax.Precision.HIGHEST)
        cr = jnp.dot(tri_up, gb, preferred_element_type=F32, precision=lax.Precision.HIGHEST)
        tot = jnp.dot(ones, gb, preferred_element_type=F32, precision=lax.Precision.HIGHEST)
        gam = jnp.where(lane < DN_HEADS, cf, cr)
        tot_sh = pltpu.roll(tot, 2 * nh2, 1)
        blk = jnp.where(lane < nh2, gam, jnp.where(lane < 2 * nh2, beta[r], jnp.where(lane < 3 * nh2, tot_sh, 0.0)))
        a_ref[r, :] = blk
        at_ref[c] = blk.T[:4 * nh2, :]


def _delta_aux(ab, a_log, dt_bias, layer):
    ns, seq, _ = ab.shape
    nblk = seq // DN_BLOCK
    nh2 = 2 * DN_HEADS

    def row(p):
        return jnp.concatenate([p.reshape(DEPTH, 1, nh2), jnp.zeros((DEPTH, 1, AUX_W - nh2), F32)], -1)

    return pl.pallas_call(
        _delta_aux_kernel,
        grid=(ns,),
        in_specs=[pl.BlockSpec((None, seq, AUX_W), lambda s: (s, 0, 0)),
                  _layer_spec(layer, (1, AUX_W)), _layer_spec(layer, (1, AUX_W))],
        out_specs=[pl.BlockSpec((None, seq, AUX_W), lambda s: (s, 0, 0)),
                   pl.BlockSpec((None, nblk, 4 * nh2, DN_BLOCK), lambda s: (s, 0, 0, 0))],
        out_shape=[jax.ShapeDtypeStruct((ns, seq, AUX_W), F32),
                   jax.ShapeDtypeStruct((ns, nblk, 4 * nh2, DN_BLOCK), F32)],
        compiler_params=_cparams(1),
        name="delta_aux",
    )(ab, row(a_log), row(dt_bias))


_CONV_SUB = 256
_CONV_HALO = 8


def _delta_conv_kernel(x_ref, w_ref, o_ref, xp_ref, *, seq_ctx):
    s = pl.program_id(0)
    j = pl.program_id(1)
    seq = x_ref.shape[0]
    width = x_ref.shape[1]
    zeros = jnp.zeros((_CONV_HALO, width), F32)
    xp_ref[0:_CONV_HALO, :] = zeros
    xp_ref[_CONV_HALO + seq:, :] = zeros
    xp_ref[_CONV_HALO:_CONV_HALO + seq, :] = x_ref[...].astype(F32)
    is_ctx = s == pl.num_programs(0) - 1
    pad = DN_CONV // 2
    edge = _CONV_HALO
    assert seq_ctx == _CONV_SUB
    row_e = lax.broadcasted_iota(jnp.int32, (edge, 1), 0)

    def taps(start, nrows, ok_fn):
        acc = jnp.zeros((nrows, width), F32)
        for k in range(DN_CONV):
            xs = xp_ref[_CONV_HALO + start + k - pad:_CONV_HALO + start + k - pad + nrows, :]
            if ok_fn is not None:
                xs = jnp.where(ok_fn(k - pad), xs, 0.0)
            acc = acc + xs * w_ref[k:k + 1, :]
        return acc

    not_ctx = jnp.logical_not(is_ctx)
    for t in range(seq // _CONV_SUB):
        r0 = t * _CONV_SUB
        mid = taps(r0, _CONV_SUB, None)
        top = taps(r0, edge, lambda sh: not_ctx | (row_e + sh >= 0))
        bot = taps(r0 + _CONV_SUB - edge, edge, lambda sh: not_ctx | (row_e + sh < edge))
        acc = jnp.concatenate([top, mid[edge:_CONV_SUB - edge], bot], axis=0)
        y = _silu(acc)
        for hd in range(width // DN_HEAD_DIM):
            cs = slice(hd * DN_HEAD_DIM, (hd + 1) * DN_HEAD_DIM)
            yh = y[:, cs]
            inv = lax.rsqrt(jnp.sum(yh * yh, axis=-1, keepdims=True) + RMS_EPS)
            fac = jnp.where(j == 0, inv * DN_HEAD_DIM ** -0.5, jnp.where(j == 1, inv, 1.0))
            o_ref[r0:r0 + _CONV_SUB, cs] = (yh * fac).astype(o_ref.dtype)


def _delta_conv(qkv, conv_w, layer, seq_ctx):
    ns, seq, _ = qkv.shape
    blk = pl.BlockSpec((None, seq, DN_WIDTH), lambda s, j: (s, 0, j))
    return pl.pallas_call(
        functools.partial(_delta_conv_kernel, seq_ctx=seq_ctx),
        grid=(ns, 3),
        in_specs=[blk, pl.BlockSpec((None, DN_CONV, DN_WIDTH), lambda s, j: (layer, 0, j))],
        out_specs=blk,
        out_shape=jax.ShapeDtypeStruct((ns, seq, 3 * DN_WIDTH), BF16),
        scratch_shapes=[pltpu.VMEM((seq + 2 * _CONV_HALO, DN_WIDTH), F32)],
        compiler_params=_cparams(2),
        name="delta_conv",
    )(qkv, conv_w)


def _delta_step(views, s_ref, i, nblk):
    nh2 = 2 * DN_HEADS
    n = DN_BLOCK
    ids = [(g, d, hd) for g in range(len(views)) for d in range(2) for hd in range(DN_HEADS)]
    bis = [i, nblk - 1 - i]
    ablk = [[vw[3](bi) for bi in bis] for vw in views]
    atblk = [[vw[4](bi) for bi in bis] for vw in views]

    r_i = lax.broadcasted_iota(jnp.int32, (n, n), 0)
    c_i = lax.broadcasted_iota(jnp.int32, (n, n), 1)
    x_i = r_i ^ c_i
    eye = (r_i == c_i).astype(F32)
    incl = [r_i >= c_i, r_i <= c_i]
    strict = [r_i > c_i, r_i < c_i]

    def col(g, d, hd, grp):
        c = grp * nh2 + d * DN_HEADS + hd
        return ablk[g][d][:, c:c + 1]

    def row(g, d, hd, grp):
        c = grp * nh2 + d * DN_HEADS + hd
        return atblk[g][d][c:c + 1, :]

    q = [views[g][0](bis[d], hd).astype(F32) for g, d, hd in ids]
    k = [views[g][1](bis[d], hd).astype(F32) for g, d, hd in ids]
    v = [views[g][2](bis[d], hd).astype(F32) for g, d, hd in ids]
    gam_c = [col(g, d, hd, 0) for g, d, hd in ids]
    beta_c = [col(g, d, hd, 1) for g, d, hd in ids]
    tot_c = [col(g, d, hd, 2) for g, d, hd in ids]
    decay = [jnp.where(incl[d], jnp.exp(jnp.where(incl[d], col(g, d, hd, 0) - row(g, d, hd, 0), 0.0)), 0.0)
             for g, d, hd in ids]
    kb = [kj * bj for kj, bj in zip(k, beta_c)]
    kf = [kj.astype(BF16) for kj in k]
    kk = [_dot_nt(kbj.astype(BF16), kfj) for kbj, kfj in zip(kb, kf)]
    qk = [_dot_nt(qj.astype(BF16), kfj) for qj, kfj in zip(q, kf)]
    m = [jnp.where(strict[d], kkj * dj, 0.0) for (_, d, _), kkj, dj in zip(ids, kk, decay)]
    qkd = [(qkj * dj).astype(BF16) for qkj, dj in zip(qk, decay)]

    t = [eye - jnp.where(x_i < 2, mj, 0.0) for mj in m]
    s = 2
    while s < n:
        sel = (x_i >= s) & (x_i < 2 * s)
        tb = [tj.astype(BF16) for tj in t]
        y = [_dot(jnp.where(sel, mj, 0.0).astype(BF16), tbj).astype(BF16) for mj, tbj in zip(m, tb)]
        z = [_dot(tbj, yj) for tbj, yj in zip(tb, y)]
        t = [tj - zj for tj, zj in zip(t, z)]
        s *= 2

    rhs = [jnp.concatenate([vj * bj, kbj * jnp.exp(gj)], axis=1).astype(BF16)
           for vj, bj, kbj, gj in zip(v, beta_c, kb, gam_c)]
    uw = [_dot(tj.astype(BF16), rj) for tj, rj in zip(t, rhs)]
    kd_t = [(kj * jnp.exp(tc - gj)).T.astype(BF16) for kj, tc, gj in zip(k, tot_c, gam_c)]

    s_prev = [s_ref[(g * 2 + d) * DN_HEADS + hd] for g, d, hd in ids]
    lhs = [jnp.concatenate([uwj[:, DN_HEAD_DIM:], qj * jnp.exp(gj)], axis=0).astype(BF16)
           for uwj, qj, gj in zip(uw, q, gam_c)]
    ws_qs = [_dot(lj, sj.astype(BF16)) for lj, sj in zip(lhs, s_prev)]
    vnb = [(uwj[:, :DN_HEAD_DIM] - wq[:n]).astype(BF16) for uwj, wq in zip(uw, ws_qs)]
    o = [wq[n:] + _dot(qkj, vj) for wq, qkj, vj in zip(ws_qs, qkd, vnb)]
    s_new = [sj * jnp.exp(row(g, d, hd, 2)) + _dot(kdj, vj)
             for (g, d, hd), sj, kdj, vj in zip(ids, s_prev, kd_t, vnb)]
    for (g, d, hd), oj, sj in zip(ids, o, s_new):
        s_ref[(g * 2 + d) * DN_HEADS + hd] = sj
        views[g][5](bis[d], hd, oj)


def _delta_kernel(qx_ref, kx_ref, vx_ref, ax_ref, atx_ref, qc_ref, kc_ref, vc_ref, ac_ref, atc_ref,
                  ox_ref, oc_ref, s_ref):
    s_ref[...] = jnp.zeros(s_ref.shape, F32)
    ox_ref[...] = jnp.zeros(ox_ref.shape, F32)
    oc_ref[...] = jnp.zeros(oc_ref.shape, F32)
    n = DN_BLOCK
    group, seq_ctx = oc_ref.shape[0], oc_ref.shape[1]

    def cols(hd):
        return slice(hd * DN_HEAD_DIM, (hd + 1) * DN_HEAD_DIM)

    def rows(bi, base=0):
        return pl.ds(pl.multiple_of(base + bi * n, n), n)

    def latent_view(g):
        def add(bi, hd, val):
            ox_ref[g, rows(bi), cols(hd)] += val
        return (lambda bi, hd: qx_ref[g, rows(bi), cols(hd)], lambda bi, hd: kx_ref[g, rows(bi), cols(hd)],
                lambda bi, hd: vx_ref[g, rows(bi), cols(hd)], lambda bi: ax_ref[g, rows(bi), :],
                lambda bi: atx_ref[g, bi], add)

    def context_view(g):
        base = g * seq_ctx

        def add(bi, hd, val):
            oc_ref[g, rows(bi), cols(hd)] += val
        return (lambda bi, hd: qc_ref[rows(bi, base), cols(hd)], lambda bi, hd: kc_ref[rows(bi, base), cols(hd)],
                lambda bi, hd: vc_ref[rows(bi, base), cols(hd)], lambda bi: ac_ref[rows(bi, base), :],
                lambda bi: atc_ref[base // n + bi], add)

    def scan(views, nblk):
        def body(i, carry):
            _delta_step(views, s_ref, i, nblk)
            return carry
        lax.fori_loop(0, nblk, body, 0)

    scan([context_view(g) for g in range(group)], seq_ctx // n)
    scan([latent_view(g) for g in range(group)], ox_ref.shape[1] // n)


DN_GROUP = 2


def _delta_rule(qkvn, aux, aux_t, batch, seq_ctx):
    ns, seq, _ = qkvn.shape
    nblk = seq // DN_BLOCK
    nh2 = 2 * DN_HEADS
    grp = DN_GROUP if batch % DN_GROUP == 0 else 1
    one = pl.Buffered(1)

    def lat(j):
        return pl.BlockSpec((grp, seq, DN_WIDTH), lambda b: (b, 0, j), pipeline_mode=one)

    def ctx(j):
        return pl.BlockSpec((None, grp * seq_ctx, DN_WIDTH), lambda b: (batch, b, j))

    return pl.pallas_call(
        _delta_kernel,
        grid=(batch // grp,),
        in_specs=[lat(0), lat(1), lat(2),
                  pl.BlockSpec((grp, seq, AUX_W), lambda b: (b, 0, 0)),
                  pl.BlockSpec((grp, nblk, 4 * nh2, DN_BLOCK), lambda b: (b, 0, 0, 0)),
                  ctx(0), ctx(1), ctx(2),
                  pl.BlockSpec((None, grp * seq_ctx, AUX_W), lambda b: (batch, b, 0)),
                  pl.BlockSpec((None, grp * seq_ctx // DN_BLOCK, 4 * nh2, DN_BLOCK), lambda b: (batch, b, 0, 0))],
        out_specs=[pl.BlockSpec((grp, seq, DN_WIDTH), lambda b: (b, 0, 0)),
                   pl.BlockSpec((grp, seq_ctx, DN_WIDTH), lambda b: (b, 0, 0))],
        out_shape=[jax.ShapeDtypeStruct((batch, seq, DN_WIDTH), F32),
                   jax.ShapeDtypeStruct((batch, seq_ctx, DN_WIDTH), F32)],
        scratch_shapes=[pltpu.VMEM((grp * nh2, DN_HEAD_DIM, DN_HEAD_DIM), F32)],
        compiler_params=_cparams(1),
        name="delta_rule",
    )(qkvn, qkvn, qkvn, aux, aux_t, qkvn, qkvn, qkvn, aux, aux_t)


def _rope_tables(seq):
    rows = seq // GRID_W
    row = jnp.repeat(jnp.arange(rows), GRID_W).astype(F32)
    col = jnp.tile(jnp.arange(GRID_W), rows).astype(F32)
    half = MLA_ROPE // 2
    inv_freq = 1.0 / (ROPE_BASE ** (jnp.arange(0, half, 2, dtype=F32) / half))
    ang_r = row[:, None] * inv_freq
    ang_c = col[:, None] * inv_freq
    ang = jnp.concatenate([ang_r, ang_r, ang_c, ang_c], axis=-1)

    def pad(t, fill):
        return jnp.concatenate([jnp.full((seq, MLA_NOPE), fill, F32), t,
                                jnp.full((seq, HEAD_PAD - MLA_QK), fill, F32)], -1)

    lat = jnp.stack([pad(jnp.cos(ang), 1.0), pad(jnp.sin(ang), 0.0)])
    ident = jnp.stack([jnp.ones((seq, HEAD_PAD), F32), jnp.zeros((seq, HEAD_PAD), F32)])
    return jnp.stack([lat, ident])


def _head_norm_rope(xh, xh_rot, gain_cos, gain_sin, scale):
    ms = jnp.sum(xh * xh, axis=-1, keepdims=True) * (1.0 / MLA_QK)
    return (xh * gain_cos + xh_rot * gain_sin) * (lax.rsqrt(ms + RMS_EPS) * scale)


def _mla_prep_kernel(cq_ref, ckv_ref, kr_ref, krr_ref, rope_ref, qn_ref, wuq_ref, wuqr_ref, kvn_ref, wuk_ref,
                     wuv_ref, gq_ref, gqr_ref, gk_ref, gkr_ref, q_ref, k_ref, v_ref):
    cos, sin = rope_ref[0], rope_ref[1]
    gq_cos, gq_sin = gq_ref[...] * cos, gqr_ref[...] * sin
    gk_cos, gk_sin = gk_ref[...] * cos, gkr_ref[...] * sin
    cq = cq_ref[...]
    cqn = (cq * lax.rsqrt(jnp.mean(cq * cq, axis=-1, keepdims=True) + RMS_EPS) * qn_ref[...]).astype(BF16)
    q = _dot(cqn, wuq_ref[...])
    q_rot = _dot(cqn, wuqr_ref[...])
    ckv = ckv_ref[...]
    ckvn = (ckv * lax.rsqrt(jnp.mean(ckv * ckv, axis=-1, keepdims=True) + RMS_EPS) * kvn_ref[...]).astype(BF16)
    kn = _dot(ckvn, wuk_ref[...])
    lane = lax.broadcasted_iota(jnp.int32, (1, MLA_HEADS * HEAD_PAD), 1)
    ones_pad = jnp.where(lane % HEAD_PAD >= MLA_V, 1.0, 0.0)
    v_ref[...] = (_dot(ckvn, wuv_ref[...]) + ones_pad).astype(v_ref.dtype)
    kr = kr_ref[...]
    kr_rot_sin = krr_ref[...] * gk_sin
    for hd in range(MLA_HEADS):
        cs = slice(hd * HEAD_PAD, (hd + 1) * HEAD_PAD)
        qh = _head_norm_rope(q[:, cs], q_rot[:, cs], gq_cos, gq_sin, MLA_QK ** -0.5 * _LOG2E)
        q_ref[:, cs] = qh.astype(q_ref.dtype)
        xk = kn[:, cs] + kr
        ms = jnp.sum(xk * xk, axis=-1, keepdims=True) * (1.0 / MLA_QK)
        k_ref[:, cs] = ((xk * gk_cos + kr_rot_sin) * lax.rsqrt(ms + RMS_EPS)).astype(k_ref.dtype)


def _pad_heads(w, valid, lo=0):
    lead = w.shape[:-1]
    w = w.reshape(lead + (MLA_HEADS, valid))
    parts = []
    if lo:
        parts.append(jnp.zeros(lead + (MLA_HEADS, lo), w.dtype))
    parts.append(w)
    parts.append(jnp.zeros(lead + (MLA_HEADS, HEAD_PAD - lo - valid), w.dtype))
    return jnp.concatenate(parts, -1).reshape(lead + (MLA_HEADS * HEAD_PAD,))


def _pad_gain(g):
    return jnp.concatenate([g, jnp.zeros(g.shape[:-1] + (HEAD_PAD - MLA_QK,), g.dtype)], -1)[:, None, :]


def _pad_rot_gain(g):
    lead = g.shape[:-1]
    return jnp.concatenate([jnp.zeros(lead + (MLA_NOPE,), g.dtype), _rot_half_gain(g[..., MLA_NOPE:]),
                            jnp.zeros(lead + (HEAD_PAD - MLA_QK,), g.dtype)], -1)[:, None, :]


def _mla_prep(cq, ckv, kr, krr, rope, q_norm, w_uq_p, w_uq_rot, kv_norm, w_uk_p, w_uv, gq, gq_rot, gk, gk_rot,
              layer):
    ns, seq, _ = cq.shape
    tm = min(ROW_TILE, seq)

    def tok(width):
        return pl.BlockSpec((None, tm, width), lambda s, t: (s, t, 0))

    wq = MLA_HEADS * HEAD_PAD
    gain = _layer_spec(layer, (1, HEAD_PAD))
    return pl.pallas_call(
        _mla_prep_kernel,
        grid=(ns, seq // tm),
        in_specs=[tok(MLA_Q_LORA), tok(MLA_KV_LORA), tok(HEAD_PAD), tok(HEAD_PAD),
                  pl.BlockSpec((None, 2, tm, HEAD_PAD), lambda s, t: (s // (ns - 1), 0, t, 0)),
                  _layer_spec(layer, (1, MLA_Q_LORA)), _layer_spec(layer, (MLA_Q_LORA, wq)),
                  _layer_spec(layer, (MLA_Q_LORA, wq)),
                  _layer_spec(layer, (1, MLA_KV_LORA)), _layer_spec(layer, (MLA_KV_LORA, wq)),
                  _layer_spec(layer, (MLA_KV_LORA, wq)),
                  gain, gain, gain, gain],
        out_specs=[tok(wq), tok(wq), tok(wq)],
        out_shape=[jax.ShapeDtypeStruct((ns, seq, wq), BF16), jax.ShapeDtypeStruct((ns, seq, wq), BF16),
                   jax.ShapeDtypeStruct((ns, seq, wq), BF16)],
        compiler_params=_cparams(2),
        name="mla_prep",
    )(cq, ckv, kr, krr, rope, q_norm.reshape(DEPTH, 1, MLA_Q_LORA), w_uq_p, w_uq_rot,
      kv_norm.reshape(DEPTH, 1, MLA_KV_LORA), w_uk_p, w_uv, gq, gq_rot, gk, gk_rot)


def _attn_kernel(*refs, n_src):
    q_ref = refs[0]
    k_refs = refs[1:1 + n_src]
    v_refs = refs[1 + n_src:1 + 2 * n_src]
    o_ref = refs[1 + 2 * n_src]
    n_heads = q_ref.shape[1] // HEAD_PAD

    def cols(hh):
        return slice(hh * HEAD_PAD, (hh + 1) * HEAD_PAD)

    def score(hh):
        return [_dot_nt(q_ref[:, cols(hh)], k_ref[:, cols(hh)]) for k_ref in k_refs]

    def probs(scores):
        m = scores[0].max(axis=-1, keepdims=True)
        for sc in scores[1:]:
            m = jnp.maximum(m, sc.max(axis=-1, keepdims=True))
        return [jnp.exp2(sc - m).astype(BF16) for sc in scores]

    def weighted(hh, ps):
        acc = _dot(ps[0], v_refs[0][:, cols(hh)])
        for p, v_ref in zip(ps[1:], v_refs[1:]):
            acc = acc + _dot(p, v_ref[:, cols(hh)])
        return acc / acc[:, MLA_V:MLA_V + 1]

    ahead = 2
    scores = {hh: score(hh) for hh in range(min(ahead, n_heads))}
    heads = []
    for hh in range(n_heads):
        ps = probs(scores.pop(hh))
        if hh + ahead < n_heads:
            scores[hh + ahead] = score(hh + ahead)
        heads.append(weighted(hh, ps))
    lane = lax.broadcasted_iota(jnp.int32, (q_ref.shape[0], HEAD_PAD), 1)
    for pr in range(n_heads // 2):
        out = jnp.where(lane < MLA_V, heads[2 * pr], pltpu.roll(heads[2 * pr + 1], MLA_V, 1))
        o_ref[:, pr * 2 * MLA_V:(pr + 1) * 2 * MLA_V] = out.astype(o_ref.dtype)


ATTN_Q_TILE = 512
ATTN_HEADS_PER_STEP = 4


def _attn_latent(q, k, v, batch, seq_ctx):
    ns, seq, _ = q.shape
    tq = min(ATTN_Q_TILE, seq)
    pair = ATTN_HEADS_PER_STEP * HEAD_PAD
    return pl.pallas_call(
        functools.partial(_attn_kernel, n_src=2),
        grid=(batch, MLA_HEADS // ATTN_HEADS_PER_STEP, seq // tq),
        in_specs=[pl.BlockSpec((None, tq, pair), lambda b, h, t: (b, t, h)),
                  pl.BlockSpec((None, seq, pair), lambda b, h, t: (b, 0, h)),
                  pl.BlockSpec((None, seq_ctx, pair), lambda b, h, t: (batch, b, h)),
                  pl.BlockSpec((None, seq, pair), lambda b, h, t: (b, 0, h)),
                  pl.BlockSpec((None, seq_ctx, pair), lambda b, h, t: (batch, b, h))],
        out_specs=pl.BlockSpec((None, tq, ATTN_HEADS_PER_STEP * MLA_V), lambda b, h, t: (b, t, h)),
        out_shape=jax.ShapeDtypeStruct((ns, seq, MLA_WIDTH), BF16),
        compiler_params=_cparams(3),
        name="attn_latent",
    )(q, k, k, v, v)


def _attn_context(q, k, v, yc, batch, seq_ctx):
    ns, seq, _ = q.shape
    pair = ATTN_HEADS_PER_STEP * HEAD_PAD

    def kern(q_ref, k_ref, v_ref, yc_in_ref, o_ref):
        del yc_in_ref
        _attn_kernel(q_ref, k_ref, v_ref, o_ref, n_src=1)

    return pl.pallas_call(
        kern,
        grid=(batch, MLA_HEADS // ATTN_HEADS_PER_STEP),
        in_specs=[pl.BlockSpec((None, seq_ctx, pair), lambda b, h: (batch, b, h)),
                  pl.BlockSpec((None, seq_ctx, pair), lambda b, h: (batch, b, h)),
                  pl.BlockSpec((None, seq_ctx, pair), lambda b, h: (batch, b, h)),
                  pl.BlockSpec(memory_space=pl.ANY)],
        out_specs=pl.BlockSpec((None, seq_ctx, ATTN_HEADS_PER_STEP * MLA_V), lambda b, h: (batch, b, h)),
        out_shape=jax.ShapeDtypeStruct(yc.shape, yc.dtype),
        input_output_aliases={3: 0},
        compiler_params=_cparams(2),
        name="attn_context",
    )(q, k, v, yc)


def _merge_kernel(h_ref, mod_ref, ya_ref, ox_ref, oc_ref, z_ref, yc_ref, gate_ref, onw_ref,
                  wa_ref, wb_ref, wc_ref, wo_ref, out_ref, *, n_latent):
    d = h_ref.shape[1]
    o = jnp.where(pl.program_id(0) < n_latent, ox_ref[...], oc_ref[...])
    z = z_ref[...].astype(F32)
    yb_parts = []
    for hd in range(DN_HEADS):
        cs = slice(hd * DN_HEAD_DIM, (hd + 1) * DN_HEAD_DIM)
        oh = o[:, cs]
        ohn = oh * lax.rsqrt(jnp.mean(oh * oh, axis=-1, keepdims=True) + RMS_EPS) * onw_ref[...]
        yb_parts.append(ohn * _silu(z[:, cs]))
    yb = jnp.concatenate(yb_parts, axis=1).astype(BF16)
    m = (_sigmoid(gate_ref[:, 0:d].astype(F32)) * _dot(ya_ref[...], wa_ref[...])
         + _sigmoid(gate_ref[:, d:2 * d].astype(F32)) * _dot(yb, wb_ref[...])
         + _sigmoid(gate_ref[:, 2 * d:3 * d].astype(F32)) * _dot(yc_ref[...], wc_ref[...]))
    y = _dot(m.astype(BF16), wo_ref[...])
    out_ref[...] = h_ref[...] + mod_ref[5:6, :] * y


def _merge(h, mods, ya, o_x, o_c, z, yc, gate, out_norm, w_a, w_b, w_c, w_o, layer, n_samples):
    ns, seq, d = h.shape
    tm = min(ROW_TILE, seq)

    def tok(width):
        return pl.BlockSpec((None, tm, width), lambda s, t: (s, t, 0))

    batch = o_x.shape[0]
    last_t = seq // tm - 1
    ox_spec = pl.BlockSpec((None, tm, DN_WIDTH),
                           lambda s, t: (jnp.minimum(s, batch - 1), jnp.where(s < batch, t, last_t), 0))
    oc_spec = pl.BlockSpec((None, tm, DN_WIDTH), lambda s, t: (0, jnp.where(s < batch, 0, t), 0))
    return pl.pallas_call(
        functools.partial(_merge_kernel, n_latent=batch),
        grid=(n_samples, seq // tm),
        in_specs=[tok(d), _mod_spec(layer), tok(GM_WIDTH), ox_spec, oc_spec, tok(DN_WIDTH), tok(MLA_WIDTH),
                  tok(N_BRANCH * d), _layer_spec(layer, (1, DN_HEAD_DIM)),
                  _layer_spec(layer, (GM_WIDTH, d)), _layer_spec(layer, (DN_WIDTH, d)),
                  _layer_spec(layer, (MLA_WIDTH, d)), _layer_spec(layer, (d, d))],
        out_specs=tok(d),
        out_shape=jax.ShapeDtypeStruct((n_samples, seq, d), F32),
        compiler_params=_cparams(2),
        name="merge",
    )(h, mods, ya, o_x, o_c.reshape(1, seq, DN_WIDTH), z, yc, gate, out_norm.reshape(DEPTH, 1, DN_HEAD_DIM),
      w_a, w_b, w_c, w_o)


def kernel(x, c, ctx, c_ctx, ada_w, ada_b, ffn1_norm, ffn1_w_gu, ffn1_w_down, mix_norm, w_in, gm_ln, gm_ws, gm_bs, dn_conv, dn_a_log, dn_dt_bias, dn_out_norm, mla_q_norm, mla_w_uq, mla_kv_norm, mla_w_ukv, mla_qk_norm_q, mla_qk_norm_k, w_branch_gm, w_branch_dn, w_branch_mla, w_out, ffn2_norm, ffn2_w_gu, ffn2_w_down):
    batch, seq, d = x.shape
    seq_ctx = ctx.shape[1]
    assert d == D_MODEL and batch * seq_ctx == seq and batch + 1 <= MOD_ROWS
    assert seq % ROW_TILE == 0 or seq < ROW_TILE
    ns = batch + 1

    cond = jnp.concatenate([c, c_ctx[None, :], jnp.zeros((MOD_ROWS - ns, d), F32)], 0)
    mods = _ada_table(cond, ada_w, ada_b)
    h = jnp.concatenate([x, ctx.reshape(1, seq, d)], 0)

    bf = lambda w: w.astype(BF16)
    ffn1_gu, ffn1_dn, ffn2_gu, ffn2_dn = bf(ffn1_w_gu), bf(ffn1_w_down), bf(ffn2_w_gu), bf(ffn2_w_down)
    w_in_p = _pack_w_in(w_in)
    gm_ws_b = bf(gm_ws)
    gm_bs_t = jnp.swapaxes(gm_bs, 1, 2)
    w_uq_p = bf(_pad_heads(mla_w_uq, MLA_QK))
    ukv = mla_w_ukv.reshape(DEPTH, MLA_KV_LORA, MLA_HEADS, MLA_NOPE + MLA_V)
    w_uk_p = bf(_pad_heads(ukv[..., :MLA_NOPE].reshape(DEPTH, MLA_KV_LORA, MLA_HEADS * MLA_NOPE), MLA_NOPE))
    w_uv = bf(_pad_heads(ukv[..., MLA_NOPE:].reshape(DEPTH, MLA_KV_LORA, MLA_WIDTH), MLA_V))
    uq_rope = mla_w_uq.reshape(DEPTH, MLA_Q_LORA, MLA_HEADS, MLA_QK)[..., MLA_NOPE:]
    w_uq_rot = bf(_pad_heads(_rot_half_cols(uq_rope).reshape(DEPTH, MLA_Q_LORA, MLA_HEADS * MLA_ROPE),
                             MLA_ROPE, lo=MLA_NOPE))
    gq, gk = _pad_gain(mla_qk_norm_q), _pad_gain(mla_qk_norm_k)
    gq_rot, gk_rot = _pad_rot_gain(mla_qk_norm_q), _pad_rot_gain(mla_qk_norm_k)
    w_a, w_b, w_c, w_o = bf(w_branch_gm), bf(w_branch_dn), bf(w_branch_mla), bf(w_out)
    rope = _rope_tables(seq)

    for l in range(DEPTH):
        need_ctx = l < DEPTH - 1
        h = _ffn_half(h, mods, ffn1_norm, ffn1_gu, ffn1_dn, l, 0, ns)
        ya, qkv, z, cq, ckv, gate, ab, kr, krr = _mixer_in(h, mods, mix_norm, w_in_p, gm_ln, gm_ws_b, gm_bs_t, l)
        aux, aux_t = _delta_aux(ab, dn_a_log, dn_dt_bias, l)
        qkvn = _delta_conv(qkv, dn_conv, l, seq_ctx)
        o_x, o_c = _delta_rule(qkvn, aux, aux_t, batch, seq_ctx)
        qh, kh, vh = _mla_prep(cq, ckv, kr, krr, rope, mla_q_norm, w_uq_p, w_uq_rot, mla_kv_norm, w_uk_p, w_uv,
                               gq, gq_rot, gk, gk_rot, l)
        yc = _attn_latent(qh, kh, vh, batch, seq_ctx)
        n_out = ns if need_ctx else batch
        if need_ctx:
            yc = _attn_context(qh, kh, vh, yc, batch, seq_ctx)
        h = _merge(h, mods, ya, o_x, o_c, z, yc, gate, dn_out_norm, w_a, w_b, w_c, w_o, l, n_out)
        h = _ffn_half(h, mods, ffn2_norm, ffn2_gu, ffn2_dn, l, 6, n_out)
    return h[:batch]
```

```python
import functools
import math

import jax
import jax.numpy as jnp
from jax import lax
from jax.experimental import pallas as pl
from jax.experimental.pallas import tpu as pltpu

F32 = jnp.float32
BF16 = jnp.bfloat16

D_MODEL = 1024
DEPTH = 4
GRID_W = 64
CTX_LEN = 256
RMS_EPS = 1e-6
N_MOD = 9
FFN_HIDDEN = 2816

GM_GROUPS = 4
GM_WIDTH = 512
GM_CHUNK = 128

DN_HEADS = 4
DN_HEAD_DIM = 128
DN_WIDTH = DN_HEADS * DN_HEAD_DIM
DN_CONV = 5
DN_BLOCK = 128

MLA_HEADS = 8
MLA_NOPE = 64
MLA_ROPE = 32
MLA_QK = MLA_NOPE + MLA_ROPE
MLA_V = 64
MLA_WIDTH = MLA_HEADS * MLA_V
MLA_Q_LORA = 384
MLA_KV_LORA = 256
ROPE_BASE = 10000.0
N_BRANCH = 3

LANE = 128
HEAD_PAD = 128
MOD_ROWS = 16
AUX_W = 128
VMEM_LIMIT = 56 * 1024 * 1024

_LOG2E = 1.4426950408889634
FFN_CHUNK = 256
ROW_TILE = 512
MIX_TILE = 512


def _cparams(n_axes):
    return pltpu.CompilerParams(dimension_semantics=("arbitrary",) * n_axes,
                                vmem_limit_bytes=VMEM_LIMIT)


def _dot(a, b):
    return jnp.dot(a, b, preferred_element_type=F32)


def _dot_nt(a, b):
    return lax.dot_general(a, b, (((1,), (1,)), ((), ())), preferred_element_type=F32)


def _sigmoid(x):
    return 1.0 / (1.0 + jnp.exp(-x))


def _silu(x):
    return x * _sigmoid(x)


def _gelu_tanh(x):
    return 0.5 * x * (1.0 + jnp.tanh(math.sqrt(2.0 / math.pi) * (x + 0.044715 * (x * x * x))))


def _const_spec(shape, single=True):
    nd = len(shape)
    kw = {"pipeline_mode": pl.Buffered(1)} if single else {}
    return pl.BlockSpec(shape, lambda *_: (0,) * nd, **kw)


def _layer_spec(layer, shape):
    nd = len(shape)
    return pl.BlockSpec((None,) + tuple(shape), lambda *_: (layer,) + (0,) * nd,
                        pipeline_mode=pl.Buffered(1))


def _mod_spec(layer):
    return pl.BlockSpec((None, None, N_MOD, D_MODEL), lambda s, *_: (layer, s, 0, 0))


def _ada_kernel(x_ref, w_ref, b_ref, o_ref):
    x = x_ref[...]
    xs = _silu(x).astype(BF16)
    o_ref[...] = _dot(xs, w_ref[...].astype(BF16)) + b_ref[...]


def _ada_table(cond, ada_w, ada_b):
    nb = D_MODEL
    out = pl.pallas_call(
        _ada_kernel,
        grid=(DEPTH, N_MOD),
        in_specs=[
            pl.BlockSpec((MOD_ROWS, D_MODEL), lambda l, j: (0, 0)),
            pl.BlockSpec((None, D_MODEL, nb), lambda l, j: (l, 0, j)),
            pl.BlockSpec((None, 1, nb), lambda l, j: (l, 0, j)),
        ],
        out_specs=pl.BlockSpec((None, MOD_ROWS, nb), lambda l, j: (l, 0, j)),
        out_shape=jax.ShapeDtypeStruct((DEPTH, MOD_ROWS, N_MOD * D_MODEL), F32),
        compiler_params=_cparams(2),
        name="ada_table",
    )(cond, ada_w, ada_b.reshape(DEPTH, 1, N_MOD * D_MODEL))
    return out.reshape(DEPTH, MOD_ROWS, N_MOD, D_MODEL)


def _mod_rmsnorm(h, norm_w, shift, scale):
    ms = jnp.mean(h * h, axis=-1, keepdims=True)
    xn = h * lax.rsqrt(ms + RMS_EPS) * norm_w
    return xn * (1.0 + scale) + shift


def _ffn_kernel(*refs, k0, n_latent):
    if n_latent is None:
        h_ref, mod_ref, nw_ref, wgu_ref, wd_ref, o_ref = refs
        h = h_ref[...]
    else:
        x_ref, ctx_ref, mod_ref, nw_ref, wgu_ref, wd_ref, o_ref = refs
        h = jnp.where(pl.program_id(0) < n_latent, x_ref[...], ctx_ref[...])
    xm = _mod_rmsnorm(h, nw_ref[...], mod_ref[k0:k0 + 1, :], mod_ref[k0 + 1:k0 + 2, :]).astype(BF16)
    acc = jnp.zeros(h.shape, F32)
    for c in range(FFN_HIDDEN // FFN_CHUNK):
        lo = c * FFN_CHUNK
        g = _dot(xm, wgu_ref[:, lo:lo + FFN_CHUNK])
        u = _dot(xm, wgu_ref[:, FFN_HIDDEN + lo:FFN_HIDDEN + lo + FFN_CHUNK])
        a = (_silu(g) * u).astype(BF16)
        acc = acc + _dot(a, wd_ref[lo:lo + FFN_CHUNK, :])
    o_ref[...] = h + 0.5 * mod_ref[k0 + 2:k0 + 3, :] * acc


def _ffn_half(h, mods, norm_w, w_gu, w_down, layer, k0, n_samples, ctx=None):
    _, seq, d = h.shape
    tm = min(ROW_TILE, seq)
    tok = pl.BlockSpec((None, tm, d), lambda s, t: (s, t, 0))
    if ctx is None:
        stream, stream_specs, n_latent = (h,), [tok], None
    else:
        n_latent = h.shape[0]
        last_t = seq // tm - 1
        x_spec = pl.BlockSpec((None, tm, d),
                              lambda s, t: (jnp.minimum(s, n_latent - 1), jnp.where(s < n_latent, t, last_t), 0))
        c_spec = pl.BlockSpec((None, tm, d), lambda s, t: (0, jnp.where(s < n_latent, 0, t), 0))
        stream, stream_specs = (h, ctx), [x_spec, c_spec]
    return pl.pallas_call(
        functools.partial(_ffn_kernel, k0=k0, n_latent=n_latent),
        grid=(n_samples, seq // tm),
        in_specs=stream_specs + [_mod_spec(layer), _layer_spec(layer, (1, d)),
                                 _layer_spec(layer, (d, 2 * FFN_HIDDEN)), _layer_spec(layer, (FFN_HIDDEN, d))],
        out_specs=tok,
        out_shape=jax.ShapeDtypeStruct((n_samples, seq, d), F32),
        compiler_params=_cparams(2),
        name=f"ffn_half_k{k0}",
    )(*stream, mods, norm_w.reshape(DEPTH, 1, d), w_gu, w_down)


_C_GM = 0
_C_QKV = _C_GM + 2 * GM_WIDTH
_C_Z = _C_QKV + 3 * DN_WIDTH
_C_CQ = _C_Z + DN_WIDTH
_C_AB = _C_CQ + MLA_Q_LORA
_C_CKV = _C_AB + AUX_W
_C_GATE = _C_CKV + MLA_KV_LORA
_C_KR = _C_GATE + N_BRANCH * D_MODEL
_C_KRR = _C_KR + HEAD_PAD
_C_END = _C_KRR + HEAD_PAD


def _rot_half_cols(w):
    q4 = MLA_ROPE // 4
    a, b, c, e = (w[..., i * q4:(i + 1) * q4] for i in range(4))
    return jnp.concatenate([-b, a, -e, c], -1)


def _rot_half_gain(g):
    q4 = MLA_ROPE // 4
    a, b, c, e = (g[..., i * q4:(i + 1) * q4] for i in range(4))
    return jnp.concatenate([b, a, e, c], -1)


def _pack_w_in(w_in):
    splits = (2 * GM_WIDTH, 3 * DN_WIDTH, DN_WIDTH, 4 * DN_HEADS, MLA_Q_LORA, MLA_KV_LORA + MLA_ROPE,
              N_BRANCH * D_MODEL)
    offs = [0]
    for s in splits:
        offs.append(offs[-1] + s)
    gm, qkv, z, ab, cq, ckvr, gate = (w_in[..., offs[i]:offs[i + 1]] for i in range(7))
    ckv, kr = ckvr[..., :MLA_KV_LORA], ckvr[..., MLA_KV_LORA:]
    lead = w_in.shape[:-1]
    ab_p = jnp.concatenate([ab, jnp.zeros(lead + (AUX_W - 4 * DN_HEADS,), w_in.dtype)], -1)
    def rope_group(cols):
        return jnp.concatenate([jnp.zeros(lead + (MLA_NOPE,), w_in.dtype), cols,
                                jnp.zeros(lead + (HEAD_PAD - MLA_QK,), w_in.dtype)], -1)

    return jnp.concatenate([gm, qkv, z, cq, ab_p, ckv, gate, rope_group(kr), rope_group(_rot_half_cols(kr))],
                           -1).astype(BF16)


def _mixer_in_kernel(h_ref, mod_ref, nw_ref, w_ref, ln_ref, ws_ref, bs_ref,
                     ya_ref, qkv_ref, z_ref, cq_ref, ckv_ref, gate_ref, ab_ref, kr_ref, krr_ref):
    h = h_ref[...]
    um = _mod_rmsnorm(h, nw_ref[...], mod_ref[3:4, :], mod_ref[4:5, :]).astype(BF16)
    zz_pre = _dot(um, w_ref[:, _C_GM:_C_QKV])
    qkv_ref[...] = _dot(um, w_ref[:, _C_QKV:_C_Z]).astype(qkv_ref.dtype)
    z_ref[...] = _dot(um, w_ref[:, _C_Z:_C_CQ]).astype(z_ref.dtype)
    cq_ab = _dot(um, w_ref[:, _C_CQ:_C_CKV])
    cq_ref[...] = cq_ab[:, :MLA_Q_LORA]
    ab_ref[...] = cq_ab[:, MLA_Q_LORA:]
    ckv_ref[...] = _dot(um, w_ref[:, _C_CKV:_C_GATE])
    gate_ref[...] = _dot(um, w_ref[:, _C_GATE:_C_KR]).astype(gate_ref.dtype)
    kr_both = _dot(um, w_ref[:, _C_KR:_C_END])
    kr_ref[...] = kr_both[:, :HEAD_PAD]
    krr_ref[...] = kr_both[:, HEAD_PAD:]
    zz = _gelu_tanh(zz_pre)
    uu = zz[:, :GM_WIDTH]
    v = zz[:, GM_WIDTH:]
    mu = jnp.mean(v, axis=-1, keepdims=True)
    vc = v - mu
    var = jnp.mean(vc * vc, axis=-1, keepdims=True)
    vn = (vc * lax.rsqrt(var + RMS_EPS) * ln_ref[...]).astype(BF16)
    gw = GM_WIDTH // GM_GROUPS
    for c in range(h.shape[0] // GM_CHUNK):
        r = slice(c * GM_CHUNK, (c + 1) * GM_CHUNK)
        for g in range(GM_GROUPS):
            cs = slice(g * gw, (g + 1) * gw)
            s = _dot(ws_ref[g], vn[r, cs]) + bs_ref[:, g:g + 1]
            ya_ref[r, cs] = (uu[r, cs] * s).astype(ya_ref.dtype)


def _mixer_in(h, mods, norm_w, w_packed, gm_ln, gm_ws, gm_bs_t, layer):
    ns, seq, d = h.shape
    tm = min(MIX_TILE, seq)

    def tok(width):
        return pl.BlockSpec((None, tm, width), lambda s, t: (s, t, 0))

    def out(width, dtype=F32):
        return jax.ShapeDtypeStruct((ns, seq, width), dtype)

    return pl.pallas_call(
        _mixer_in_kernel,
        grid=(ns, seq // tm),
        in_specs=[tok(d), _mod_spec(layer), _layer_spec(layer, (1, d)), _layer_spec(layer, (d, _C_END)),
                  _layer_spec(layer, (1, GM_WIDTH)),
                  _layer_spec(layer, (GM_GROUPS, GM_CHUNK, GM_CHUNK)),
                  _layer_spec(layer, (GM_CHUNK, GM_GROUPS))],
        out_specs=[tok(GM_WIDTH), tok(3 * DN_WIDTH), tok(DN_WIDTH), tok(MLA_Q_LORA), tok(MLA_KV_LORA),
                   tok(N_BRANCH * d), tok(AUX_W), tok(HEAD_PAD), tok(HEAD_PAD)],
        out_shape=[out(GM_WIDTH, BF16), out(3 * DN_WIDTH, BF16), out(DN_WIDTH, BF16), out(MLA_Q_LORA),
                   out(MLA_KV_LORA), out(N_BRANCH * d, BF16), out(AUX_W), out(HEAD_PAD), out(HEAD_PAD)],
        compiler_params=_cparams(2),
        name="mixer_in",
    )(h, mods, norm_w.reshape(DEPTH, 1, d), w_packed, gm_ln.reshape(DEPTH, 1, GM_WIDTH), gm_ws, gm_bs_t)


def _tri(n, upper, inclusive=True):
    r = lax.broadcasted_iota(jnp.int32, (n, n), 0)
    c = lax.broadcasted_iota(jnp.int32, (n, n), 1)
    if upper:
        return (r <= c) if inclusive else (r < c)
    return (r >= c) if inclusive else (r > c)


def _delta_aux_kernel(ab_ref, alog_ref, dtb_ref, a_ref, at_ref):
    nh2 = 2 * DN_HEADS
    ab = ab_ref[...]
    sp = jnp.maximum(ab + dtb_ref[...], 0.0) + jnp.log1p(jnp.exp(-jnp.abs(ab + dtb_ref[...])))
    g = -jnp.exp(alog_ref[...]) * sp
    lane = lax.broadcasted_iota(jnp.int32, (DN_BLOCK, AUX_W), 1)
    tri_lo = _tri(DN_BLOCK, False).astype(F32)
    tri_up = _tri(DN_BLOCK, True).astype(F32)
    ones = jnp.ones((DN_BLOCK, DN_BLOCK), F32)
    beta = _sigmoid(ab)
    for c in range(ab.shape[0] // DN_BLOCK):
        r = slice(c * DN_BLOCK, (c + 1) * DN_BLOCK)
        gb = jnp.where(lane < nh2, g[r], 0.0)
        cf = jnp.dot(tri_lo, gb, preferred_element_type=F32, precision=lax.Precision.HIGHEST)
        cr = jnp.dot(tri_up, gb, preferred_element_type=F32, precision=lax.Precision.HIGHEST)
        tot = jnp.dot(ones, gb, preferred_element_type=F32, precision=lax.Precision.HIGHEST)
        gam = jnp.where(lane < DN_HEADS, cf, cr)
        tot_sh = pltpu.roll(tot, 2 * nh2, 1)
        blk = jnp.where(lane < nh2, gam, jnp.where(lane < 2 * nh2, beta[r], jnp.where(lane < 3 * nh2, tot_sh, 0.0)))
        a_ref[r, :] = blk
        at_ref[c] = blk.T[:4 * nh2, :]


def _delta_aux(ab, a_log, dt_bias, layer):
    ns, seq, _ = ab.shape
    nblk = seq // DN_BLOCK
    nh2 = 2 * DN_HEADS

    def row(p):
        return jnp.concatenate([p.reshape(DEPTH, 1, nh2), jnp.zeros((DEPTH, 1, AUX_W - nh2), F32)], -1)

    return pl.pallas_call(
        _delta_aux_kernel,
        grid=(ns,),
        in_specs=[pl.BlockSpec((None, seq, AUX_W), lambda s: (s, 0, 0)),
                  _layer_spec(layer, (1, AUX_W)), _layer_spec(layer, (1, AUX_W))],
        out_specs=[pl.BlockSpec((None, seq, AUX_W), lambda s: (s, 0, 0)),
                   pl.BlockSpec((None, nblk, 4 * nh2, DN_BLOCK), lambda s: (s, 0, 0, 0))],
        out_shape=[jax.ShapeDtypeStruct((ns, seq, AUX_W), F32),
                   jax.ShapeDtypeStruct((ns, nblk, 4 * nh2, DN_BLOCK), F32)],
        compiler_params=_cparams(1),
        name="delta_aux",
    )(ab, row(a_log), row(dt_bias))


_CONV_SUB = 256
_CONV_HALO = 8


def _delta_conv_kernel(x_ref, w_ref, o_ref, xp_ref, *, seq_ctx):
    s = pl.program_id(0)
    j = pl.program_id(1)
    seq = x_ref.shape[0]
    width = x_ref.shape[1]
    zeros = jnp.zeros((_CONV_HALO, width), F32)
    xp_ref[0:_CONV_HALO, :] = zeros
    xp_ref[_CONV_HALO + seq:, :] = zeros
    xp_ref[_CONV_HALO:_CONV_HALO + seq, :] = x_ref[...].astype(F32)
    is_ctx = s == pl.num_programs(0) - 1
    pad = DN_CONV // 2
    edge = _CONV_HALO
    assert seq_ctx == _CONV_SUB
    row_e = lax.broadcasted_iota(jnp.int32, (edge, 1), 0)

    def taps(start, nrows, ok_fn):
        acc = jnp.zeros((nrows, width), F32)
        for k in range(DN_CONV):
            xs = xp_ref[_CONV_HALO + start + k - pad:_CONV_HALO + start + k - pad + nrows, :]
            if ok_fn is not None:
                xs = jnp.where(ok_fn(k - pad), xs, 0.0)
            acc = acc + xs * w_ref[k:k + 1, :]
        return acc

    not_ctx = jnp.logical_not(is_ctx)
    for t in range(seq // _CONV_SUB):
        r0 = t * _CONV_SUB
        mid = taps(r0, _CONV_SUB, None)
        top = taps(r0, edge, lambda sh: not_ctx | (row_e + sh >= 0))
        bot = taps(r0 + _CONV_SUB - edge, edge, lambda sh: not_ctx | (row_e + sh < edge))
        acc = jnp.concatenate([top, mid[edge:_CONV_SUB - edge], bot], axis=0)
        y = _silu(acc)
        for hd in range(width // DN_HEAD_DIM):
            cs = slice(hd * DN_HEAD_DIM, (hd + 1) * DN_HEAD_DIM)
            yh = y[:, cs]
            inv = lax.rsqrt(jnp.sum(yh * yh, axis=-1, keepdims=True) + RMS_EPS)
            fac = jnp.where(j == 0, inv * DN_HEAD_DIM ** -0.5, jnp.where(j == 1, inv, 1.0))
            o_ref[r0:r0 + _CONV_SUB, cs] = (yh * fac).astype(o_ref.dtype)


def _delta_conv(qkv, conv_w, layer, seq_ctx):
    ns, seq, _ = qkv.shape
    blk = pl.BlockSpec((None, seq, DN_WIDTH), lambda s, j: (s, 0, j))
    return pl.pallas_call(
        functools.partial(_delta_conv_kernel, seq_ctx=seq_ctx),
        grid=(ns, 3),
        in_specs=[blk, pl.BlockSpec((None, DN_CONV, DN_WIDTH), lambda s, j: (layer, 0, j))],
        out_specs=blk,
        out_shape=jax.ShapeDtypeStruct((ns, seq, 3 * DN_WIDTH), BF16),
        scratch_shapes=[pltpu.VMEM((seq + 2 * _CONV_HALO, DN_WIDTH), F32)],
        compiler_params=_cparams(2),
        name="delta_conv",
    )(qkv, conv_w)


def _delta_step(views, s_ref, i, nblk):
    nh2 = 2 * DN_HEADS
    n = DN_BLOCK
    ids = [(g, d, hd) for g in range(len(views)) for d in range(2) for hd in range(DN_HEADS)]
    bis = [i, nblk - 1 - i]
    ablk = [[vw[3](bi) for bi in bis] for vw in views]
    atblk = [[vw[4](bi) for bi in bis] for vw in views]

    r_i = lax.broadcasted_iota(jnp.int32, (n, n), 0)
    c_i = lax.broadcasted_iota(jnp.int32, (n, n), 1)
    x_i = r_i ^ c_i
    eye = (r_i == c_i).astype(F32)
    incl = [r_i >= c_i, r_i <= c_i]
    strict = [r_i > c_i, r_i < c_i]

    def col(g, d, hd, grp):
        c = grp * nh2 + d * DN_HEADS + hd
        return ablk[g][d][:, c:c + 1]

    def row(g, d, hd, grp):
        c = grp * nh2 + d * DN_HEADS + hd
        return atblk[g][d][c:c + 1, :]

    q = [views[g][0](bis[d], hd).astype(F32) for g, d, hd in ids]
    k = [views[g][1](bis[d], hd).astype(F32) for g, d, hd in ids]
    v = [views[g][2](bis[d], hd).astype(F32) for g, d, hd in ids]
    gam_c = [col(g, d, hd, 0) for g, d, hd in ids]
    beta_c = [col(g, d, hd, 1) for g, d, hd in ids]
    tot_c = [col(g, d, hd, 2) for g, d, hd in ids]
    decay = [jnp.where(incl[d], jnp.exp(jnp.where(incl[d], col(g, d, hd, 0) - row(g, d, hd, 0), 0.0)), 0.0)
             for g, d, hd in ids]
    kb = [kj * bj for kj, bj in zip(k, beta_c)]
    kf = [kj.astype(BF16) for kj in k]
    kk = [_dot_nt(kbj.astype(BF16), kfj) for kbj, kfj in zip(kb, kf)]
    qk = [_dot_nt(qj.astype(BF16), kfj) for qj, kfj in zip(q, kf)]
    m = [jnp.where(strict[d], kkj * dj, 0.0) for (_, d, _), kkj, dj in zip(ids, kk, decay)]
    qkd = [(qkj * dj).astype(BF16) for qkj, dj in zip(qk, decay)]

    t = [eye - jnp.where(x_i < 2, mj, 0.0) for mj in m]
    s = 2
    while s < n:
        sel = (x_i >= s) & (x_i < 2 * s)
        tb = [tj.astype(BF16) for tj in t]
        y = [_dot(jnp.where(sel, mj, 0.0).astype(BF16), tbj).astype(BF16) for mj, tbj in zip(m, tb)]
        z = [_dot(tbj, yj) for tbj, yj in zip(tb, y)]
        t = [tj - zj for tj, zj in zip(t, z)]
        s *= 2

    rhs = [jnp.concatenate([vj * bj, kbj * jnp.exp(gj)], axis=1).astype(BF16)
           for vj, bj, kbj, gj in zip(v, beta_c, kb, gam_c)]
    uw = [_dot(tj.astype(BF16), rj) for tj, rj in zip(t, rhs)]
    kd_t = [(kj * jnp.exp(tc - gj)).T.astype(BF16) for kj, tc, gj in zip(k, tot_c, gam_c)]

    s_prev = [s_ref[(g * 2 + d) * DN_HEADS + hd] for g, d, hd in ids]
    lhs = [jnp.concatenate([uwj[:, DN_HEAD_DIM:], qj * jnp.exp(gj)], axis=0).astype(BF16)
           for uwj, qj, gj in zip(uw, q, gam_c)]
    ws_qs = [_dot(lj, sj.astype(BF16)) for lj, sj in zip(lhs, s_prev)]
    vnb = [(uwj[:, :DN_HEAD_DIM] - wq[:n]).astype(BF16) for uwj, wq in zip(uw, ws_qs)]
    o = [wq[n:] + _dot(qkj, vj) for wq, qkj, vj in zip(ws_qs, qkd, vnb)]
    s_new = [sj * jnp.exp(row(g, d, hd, 2)) + _dot(kdj, vj)
             for (g, d, hd), sj, kdj, vj in zip(ids, s_prev, kd_t, vnb)]
    for (g, d, hd), oj, sj in zip(ids, o, s_new):
        s_ref[(g * 2 + d) * DN_HEADS + hd] = sj
        views[g][5](bis[d], hd, oj)


def _delta_kernel(qx_ref, kx_ref, vx_ref, ax_ref, atx_ref, qc_ref, kc_ref, vc_ref, ac_ref, atc_ref,
                  ox_ref, oc_ref, s_ref):
    s_ref[...] = jnp.zeros(s_ref.shape, F32)
    ox_ref[...] = jnp.zeros(ox_ref.shape, ox_ref.dtype)
    oc_ref[...] = jnp.zeros(oc_ref.shape, oc_ref.dtype)
    n = DN_BLOCK
    group, seq_ctx = oc_ref.shape[0], oc_ref.shape[1]

    def cols(hd):
        return slice(hd * DN_HEAD_DIM, (hd + 1) * DN_HEAD_DIM)

    def rows(bi, base=0):
        return pl.ds(pl.multiple_of(base + bi * n, n), n)

    def latent_view(g):
        def add(bi, hd, val):
            cur = ox_ref[g, rows(bi), cols(hd)].astype(F32)
            ox_ref[g, rows(bi), cols(hd)] = (cur + val).astype(ox_ref.dtype)
        return (lambda bi, hd: qx_ref[g, rows(bi), cols(hd)], lambda bi, hd: kx_ref[g, rows(bi), cols(hd)],
                lambda bi, hd: vx_ref[g, rows(bi), cols(hd)], lambda bi: ax_ref[g, rows(bi), :],
                lambda bi: atx_ref[g, bi], add)

    def context_view(g):
        base = g * seq_ctx

        def add(bi, hd, val):
            cur = oc_ref[g, rows(bi), cols(hd)].astype(F32)
            oc_ref[g, rows(bi), cols(hd)] = (cur + val).astype(oc_ref.dtype)
        return (lambda bi, hd: qc_ref[rows(bi, base), cols(hd)], lambda bi, hd: kc_ref[rows(bi, base), cols(hd)],
                lambda bi, hd: vc_ref[rows(bi, base), cols(hd)], lambda bi: ac_ref[rows(bi, base), :],
                lambda bi: atc_ref[base // n + bi], add)

    def scan(views, nblk):
        def body(i, carry):
            _delta_step(views, s_ref, i, nblk)
            return carry
        lax.fori_loop(0, nblk, body, 0)

    scan([context_view(g) for g in range(group)], seq_ctx // n)
    scan([latent_view(g) for g in range(group)], ox_ref.shape[1] // n)


DN_GROUP = 2


def _delta_rule(qkvn, aux, aux_t, batch, seq_ctx):
    ns, seq, _ = qkvn.shape
    nblk = seq // DN_BLOCK
    nh2 = 2 * DN_HEADS
    grp = DN_GROUP if batch % DN_GROUP == 0 else 1
    one = pl.Buffered(1)

    def lat(j):
        return pl.BlockSpec((grp, seq, DN_WIDTH), lambda b: (b, 0, j), pipeline_mode=one)

    def ctx(j):
        return pl.BlockSpec((None, grp * seq_ctx, DN_WIDTH), lambda b: (batch, b, j))

    return pl.pallas_call(
        _delta_kernel,
        grid=(batch // grp,),
        in_specs=[lat(0), lat(1), lat(2),
                  pl.BlockSpec((grp, seq, AUX_W), lambda b: (b, 0, 0), pipeline_mode=one),
                  pl.BlockSpec((grp, nblk, 4 * nh2, DN_BLOCK), lambda b: (b, 0, 0, 0), pipeline_mode=one),
                  ctx(0), ctx(1), ctx(2),
                  pl.BlockSpec((None, grp * seq_ctx, AUX_W), lambda b: (batch, b, 0)),
                  pl.BlockSpec((None, grp * seq_ctx // DN_BLOCK, 4 * nh2, DN_BLOCK), lambda b: (batch, b, 0, 0))],
        out_specs=[pl.BlockSpec((grp, seq, DN_WIDTH), lambda b: (b, 0, 0)),
                   pl.BlockSpec((grp, seq_ctx, DN_WIDTH), lambda b: (b, 0, 0))],
        out_shape=[jax.ShapeDtypeStruct((batch, seq, DN_WIDTH), F32),
                   jax.ShapeDtypeStruct((batch, seq_ctx, DN_WIDTH), F32)],
        scratch_shapes=[pltpu.VMEM((grp * nh2, DN_HEAD_DIM, DN_HEAD_DIM), F32)],
        compiler_params=_cparams(1),
        name="delta_rule",
    )(qkvn, qkvn, qkvn, aux, aux_t, qkvn, qkvn, qkvn, aux, aux_t)


def _rope_tables(seq):
    rows = seq // GRID_W
    row = jnp.repeat(jnp.arange(rows), GRID_W).astype(F32)
    col = jnp.tile(jnp.arange(GRID_W), rows).astype(F32)
    half = MLA_ROPE // 2
    inv_freq = 1.0 / (ROPE_BASE ** (jnp.arange(0, half, 2, dtype=F32) / half))
    ang_r = row[:, None] * inv_freq
    ang_c = col[:, None] * inv_freq
    ang = jnp.concatenate([ang_r, ang_r, ang_c, ang_c], axis=-1)

    def pad(t, fill):
        return jnp.concatenate([jnp.full((seq, MLA_NOPE), fill, F32), t,
                                jnp.full((seq, HEAD_PAD - MLA_QK), fill, F32)], -1)

    lat = jnp.stack([pad(jnp.cos(ang), 1.0), pad(jnp.sin(ang), 0.0)])
    ident = jnp.stack([jnp.ones((seq, HEAD_PAD), F32), jnp.zeros((seq, HEAD_PAD), F32)])
    return jnp.stack([lat, ident])


def _head_norm_rope(xh, xh_rot, gain_cos, gain_sin, scale):
    ms = jnp.sum(xh * xh, axis=-1, keepdims=True) * (1.0 / MLA_QK)
    return (xh * gain_cos + xh_rot * gain_sin) * (lax.rsqrt(ms + RMS_EPS) * scale)


def _mla_prep_kernel(cq_ref, ckv_ref, kr_ref, krr_ref, rope_ref, qn_ref, wuq_ref, wuqr_ref, kvn_ref, wuk_ref,
                     wuv_ref, gq_ref, gqr_ref, gk_ref, gkr_ref, q_ref, k_ref, v_ref):
    cos, sin = rope_ref[0], rope_ref[1]
    gq_cos, gq_sin = gq_ref[...] * cos, gqr_ref[...] * sin
    gk_cos, gk_sin = gk_ref[...] * cos, gkr_ref[...] * sin
    cq = cq_ref[...]
    cqn = (cq * lax.rsqrt(jnp.mean(cq * cq, axis=-1, keepdims=True) + RMS_EPS) * qn_ref[...]).astype(BF16)
    q = _dot(cqn, wuq_ref[...])
    q_rot = _dot(cqn, wuqr_ref[...])
    ckv = ckv_ref[...]
    ckvn = (ckv * lax.rsqrt(jnp.mean(ckv * ckv, axis=-1, keepdims=True) + RMS_EPS) * kvn_ref[...]).astype(BF16)
    kn = _dot(ckvn, wuk_ref[...])
    lane = lax.broadcasted_iota(jnp.int32, (1, MLA_HEADS * HEAD_PAD), 1)
    ones_pad = jnp.where(lane % HEAD_PAD >= MLA_V, 1.0, 0.0)
    v_ref[...] = (_dot(ckvn, wuv_ref[...]) + ones_pad).astype(v_ref.dtype)
    kr = kr_ref[...]
    kr_rot_sin = krr_ref[...] * gk_sin
    for hd in range(MLA_HEADS):
        cs = slice(hd * HEAD_PAD, (hd + 1) * HEAD_PAD)
        qh = _head_norm_rope(q[:, cs], q_rot[:, cs], gq_cos, gq_sin, MLA_QK ** -0.5 * _LOG2E)
        q_ref[:, cs] = qh.astype(q_ref.dtype)
        xk = kn[:, cs] + kr
        ms = jnp.sum(xk * xk, axis=-1, keepdims=True) * (1.0 / MLA_QK)
        k_ref[:, cs] = ((xk * gk_cos + kr_rot_sin) * lax.rsqrt(ms + RMS_EPS)).astype(k_ref.dtype)


def _pad_heads(w, valid, lo=0):
    lead = w.shape[:-1]
    w = w.reshape(lead + (MLA_HEADS, valid))
    parts = []
    if lo:
        parts.append(jnp.zeros(lead + (MLA_HEADS, lo), w.dtype))
    parts.append(w)
    parts.append(jnp.zeros(lead + (MLA_HEADS, HEAD_PAD - lo - valid), w.dtype))
    return jnp.concatenate(parts, -1).reshape(lead + (MLA_HEADS * HEAD_PAD,))


def _pad_gain(g):
    return jnp.concatenate([g, jnp.zeros(g.shape[:-1] + (HEAD_PAD - MLA_QK,), g.dtype)], -1)[:, None, :]


def _pad_rot_gain(g):
    lead = g.shape[:-1]
    return jnp.concatenate([jnp.zeros(lead + (MLA_NOPE,), g.dtype), _rot_half_gain(g[..., MLA_NOPE:]),
                            jnp.zeros(lead + (HEAD_PAD - MLA_QK,), g.dtype)], -1)[:, None, :]


def _mla_prep(cq, ckv, kr, krr, rope, q_norm, w_uq_p, w_uq_rot, kv_norm, w_uk_p, w_uv, gq, gq_rot, gk, gk_rot,
              layer):
    ns, seq, _ = cq.shape
    tm = min(ROW_TILE, seq)

    def tok(width):
        return pl.BlockSpec((None, tm, width), lambda s, t: (s, t, 0))

    wq = MLA_HEADS * HEAD_PAD
    gain = _layer_spec(layer, (1, HEAD_PAD))
    return pl.pallas_call(
        _mla_prep_kernel,
        grid=(ns, seq // tm),
        in_specs=[tok(MLA_Q_LORA), tok(MLA_KV_LORA), tok(HEAD_PAD), tok(HEAD_PAD),
                  pl.BlockSpec((None, 2, tm, HEAD_PAD), lambda s, t: (s // (ns - 1), 0, t, 0)),
                  _layer_spec(layer, (1, MLA_Q_LORA)), _layer_spec(layer, (MLA_Q_LORA, wq)),
                  _layer_spec(layer, (MLA_Q_LORA, wq)),
                  _layer_spec(layer, (1, MLA_KV_LORA)), _layer_spec(layer, (MLA_KV_LORA, wq)),
                  _layer_spec(layer, (MLA_KV_LORA, wq)),
                  gain, gain, gain, gain],
        out_specs=[tok(wq), tok(wq), tok(wq)],
        out_shape=[jax.ShapeDtypeStruct((ns, seq, wq), BF16), jax.ShapeDtypeStruct((ns, seq, wq), BF16),
                   jax.ShapeDtypeStruct((ns, seq, wq), BF16)],
        compiler_params=_cparams(2),
        name="mla_prep",
    )(cq, ckv, kr, krr, rope, q_norm.reshape(DEPTH, 1, MLA_Q_LORA), w_uq_p, w_uq_rot,
      kv_norm.reshape(DEPTH, 1, MLA_KV_LORA), w_uk_p, w_uv, gq, gq_rot, gk, gk_rot)


def _attn_kernel(*refs, n_src):
    q_ref = refs[0]
    k_refs = refs[1:1 + n_src]
    v_refs = refs[1 + n_src:1 + 2 * n_src]
    o_ref = refs[1 + 2 * n_src]
    n_heads = q_ref.shape[1] // HEAD_PAD

    def cols(hh):
        return slice(hh * HEAD_PAD, (hh + 1) * HEAD_PAD)

    def score(hh):
        return [_dot_nt(q_ref[:, cols(hh)], k_ref[:, cols(hh)]) for k_ref in k_refs]

    def probs(scores):
        m = scores[0].max(axis=-1, keepdims=True)
        for sc in scores[1:]:
            m = jnp.maximum(m, sc.max(axis=-1, keepdims=True))
        return [jnp.exp2(sc - m).astype(BF16) for sc in scores]

    def weighted(hh, ps):
        acc = _dot(ps[0], v_refs[0][:, cols(hh)])
        for p, v_ref in zip(ps[1:], v_refs[1:]):
            acc = acc + _dot(p, v_ref[:, cols(hh)])
        return acc / acc[:, MLA_V:MLA_V + 1]

    ahead = 2
    scores = {hh: score(hh) for hh in range(min(ahead, n_heads))}
    heads = []
    for hh in range(n_heads):
        ps = probs(scores.pop(hh))
        if hh + ahead < n_heads:
            scores[hh + ahead] = score(hh + ahead)
        heads.append(weighted(hh, ps))
    lane = lax.broadcasted_iota(jnp.int32, (q_ref.shape[0], HEAD_PAD), 1)
    for pr in range(n_heads // 2):
        out = jnp.where(lane < MLA_V, heads[2 * pr], pltpu.roll(heads[2 * pr + 1], MLA_V, 1))
        o_ref[:, pr * 2 * MLA_V:(pr + 1) * 2 * MLA_V] = out.astype(o_ref.dtype)


ATTN_Q_TILE = 512
ATTN_HEADS_PER_STEP = 4


def _attn_latent(q, k, v, batch, seq_ctx):
    ns, seq, _ = q.shape
    tq = min(ATTN_Q_TILE, seq)
    pair = ATTN_HEADS_PER_STEP * HEAD_PAD
    return pl.pallas_call(
        functools.partial(_attn_kernel, n_src=2),
        grid=(batch, MLA_HEADS // ATTN_HEADS_PER_STEP, seq // tq),
        in_specs=[pl.BlockSpec((None, tq, pair), lambda b, h, t: (b, t, h)),
                  pl.BlockSpec((None, seq, pair), lambda b, h, t: (b, 0, h)),
                  pl.BlockSpec((None, seq_ctx, pair), lambda b, h, t: (batch, b, h)),
                  pl.BlockSpec((None, seq, pair), lambda b, h, t: (b, 0, h)),
                  pl.BlockSpec((None, seq_ctx, pair), lambda b, h, t: (batch, b, h))],
        out_specs=pl.BlockSpec((None, tq, ATTN_HEADS_PER_STEP * MLA_V), lambda b, h, t: (b, t, h)),
        out_shape=jax.ShapeDtypeStruct((ns, seq, MLA_WIDTH), BF16),
        compiler_params=_cparams(3),
        name="attn_latent",
    )(q, k, k, v, v)


def _attn_context(q, k, v, yc, batch, seq_ctx):
    ns, seq, _ = q.shape
    pair = ATTN_HEADS_PER_STEP * HEAD_PAD

    def kern(q_ref, k_ref, v_ref, yc_in_ref, o_ref):
        del yc_in_ref
        _attn_kernel(q_ref, k_ref, v_ref, o_ref, n_src=1)

    return pl.pallas_call(
        kern,
        grid=(batch, MLA_HEADS // ATTN_HEADS_PER_STEP),
        in_specs=[pl.BlockSpec((None, seq_ctx, pair), lambda b, h: (batch, b, h)),
                  pl.BlockSpec((None, seq_ctx, pair), lambda b, h: (batch, b, h)),
                  pl.BlockSpec((None, seq_ctx, pair), lambda b, h: (batch, b, h)),
                  pl.BlockSpec(memory_space=pl.ANY)],
        out_specs=pl.BlockSpec((None, seq_ctx, ATTN_HEADS_PER_STEP * MLA_V), lambda b, h: (batch, b, h)),
        out_shape=jax.ShapeDtypeStruct(yc.shape, yc.dtype),
        input_output_aliases={3: 0},
        compiler_params=_cparams(2),
        name="attn_context",
    )(q, k, v, yc)


def _merge_kernel(h_ref, mod_ref, ya_ref, ox_ref, oc_ref, z_ref, yc_ref, gate_ref, onw_ref,
                  wa_ref, wb_ref, wc_ref, wo_ref, out_ref, *, n_latent):
    d = h_ref.shape[1]
    is_latent = pl.program_id(0) < n_latent

    def gated_sum(rs):
        o = jnp.where(is_latent, ox_ref[rs, :], oc_ref[rs, :]).astype(F32)
        z = z_ref[rs, :].astype(F32)
        yb_parts = []
        for hd in range(DN_HEADS):
            cs = slice(hd * DN_HEAD_DIM, (hd + 1) * DN_HEAD_DIM)
            oh = o[:, cs]
            ohn = oh * lax.rsqrt(jnp.mean(oh * oh, axis=-1, keepdims=True) + RMS_EPS) * onw_ref[...]
            yb_parts.append(ohn * _silu(z[:, cs]))
        yb = jnp.concatenate(yb_parts, axis=1).astype(BF16)
        m = (_sigmoid(gate_ref[rs, 0:d].astype(F32)) * _dot(ya_ref[rs, :], wa_ref[...])
             + _sigmoid(gate_ref[rs, d:2 * d].astype(F32)) * _dot(yb, wb_ref[...])
             + _sigmoid(gate_ref[rs, 2 * d:3 * d].astype(F32)) * _dot(yc_ref[rs, :], wc_ref[...]))
        return m.astype(BF16)

    half = h_ref.shape[0] // 2
    parts = [slice(0, half), slice(half, 2 * half)]
    ms = [gated_sum(rs) for rs in parts]
    for rs, m in zip(parts, ms):
        out_ref[rs, :] = h_ref[rs, :] + mod_ref[5:6, :] * _dot(m, wo_ref[...])


def _merge(h, mods, ya, o_x, o_c, z, yc, gate, out_norm, w_a, w_b, w_c, w_o, layer, n_samples):
    ns, seq, d = h.shape
    tm = min(ROW_TILE, seq)

    def tok(width):
        return pl.BlockSpec((None, tm, width), lambda s, t: (s, t, 0))

    batch = o_x.shape[0]
    last_t = seq // tm - 1
    ox_spec = pl.BlockSpec((None, tm, DN_WIDTH),
                           lambda s, t: (jnp.minimum(s, batch - 1), jnp.where(s < batch, t, last_t), 0))
    oc_spec = pl.BlockSpec((None, tm, DN_WIDTH), lambda s, t: (0, jnp.where(s < batch, 0, t), 0))
    return pl.pallas_call(
        functools.partial(_merge_kernel, n_latent=batch),
        grid=(n_samples, seq // tm),
        in_specs=[tok(d), _mod_spec(layer), tok(GM_WIDTH), ox_spec, oc_spec, tok(DN_WIDTH), tok(MLA_WIDTH),
                  tok(N_BRANCH * d), _layer_spec(layer, (1, DN_HEAD_DIM)),
                  _layer_spec(layer, (GM_WIDTH, d)), _layer_spec(layer, (DN_WIDTH, d)),
                  _layer_spec(layer, (MLA_WIDTH, d)), _layer_spec(layer, (d, d))],
        out_specs=tok(d),
        out_shape=jax.ShapeDtypeStruct((n_samples, seq, d), F32),
        compiler_params=_cparams(2),
        name="merge",
    )(h, mods, ya, o_x, o_c.reshape(1, seq, DN_WIDTH), z, yc, gate, out_norm.reshape(DEPTH, 1, DN_HEAD_DIM),
      w_a, w_b, w_c, w_o)


def kernel(x, c, ctx, c_ctx, ada_w, ada_b, ffn1_norm, ffn1_w_gu, ffn1_w_down, mix_norm, w_in, gm_ln, gm_ws, gm_bs, dn_conv, dn_a_log, dn_dt_bias, dn_out_norm, mla_q_norm, mla_w_uq, mla_kv_norm, mla_w_ukv, mla_qk_norm_q, mla_qk_norm_k, w_branch_gm, w_branch_dn, w_branch_mla, w_out, ffn2_norm, ffn2_w_gu, ffn2_w_down):
    batch, seq, d = x.shape
    seq_ctx = ctx.shape[1]
    assert d == D_MODEL and batch * seq_ctx == seq and batch + 1 <= MOD_ROWS
    assert seq % ROW_TILE == 0 or seq < ROW_TILE
    ns = batch + 1

    cond = jnp.concatenate([c, c_ctx[None, :], jnp.zeros((MOD_ROWS - ns, d), F32)], 0)
    mods = _ada_table(cond, ada_w, ada_b)
    h = None

    bf = lambda w: w.astype(BF16)
    ffn1_gu, ffn1_dn, ffn2_gu, ffn2_dn = bf(ffn1_w_gu), bf(ffn1_w_down), bf(ffn2_w_gu), bf(ffn2_w_down)
    w_in_p = _pack_w_in(w_in)
    gm_ws_b = bf(gm_ws)
    gm_bs_t = jnp.swapaxes(gm_bs, 1, 2)
    w_uq_p = bf(_pad_heads(mla_w_uq, MLA_QK))
    ukv = mla_w_ukv.reshape(DEPTH, MLA_KV_LORA, MLA_HEADS, MLA_NOPE + MLA_V)
    w_uk_p = bf(_pad_heads(ukv[..., :MLA_NOPE].reshape(DEPTH, MLA_KV_LORA, MLA_HEADS * MLA_NOPE), MLA_NOPE))
    w_uv = bf(_pad_heads(ukv[..., MLA_NOPE:].reshape(DEPTH, MLA_KV_LORA, MLA_WIDTH), MLA_V))
    uq_rope = mla_w_uq.reshape(DEPTH, MLA_Q_LORA, MLA_HEADS, MLA_QK)[..., MLA_NOPE:]
    w_uq_rot = bf(_pad_heads(_rot_half_cols(uq_rope).reshape(DEPTH, MLA_Q_LORA, MLA_HEADS * MLA_ROPE),
                             MLA_ROPE, lo=MLA_NOPE))
    gq, gk = _pad_gain(mla_qk_norm_q), _pad_gain(mla_qk_norm_k)
    gq_rot, gk_rot = _pad_rot_gain(mla_qk_norm_q), _pad_rot_gain(mla_qk_norm_k)
    w_a, w_b, w_c, w_o = bf(w_branch_gm), bf(w_branch_dn), bf(w_branch_mla), bf(w_out)
    rope = _rope_tables(seq)

    for l in range(DEPTH):
        need_ctx = l < DEPTH - 1
        if l == 0:
            h = _ffn_half(x, mods, ffn1_norm, ffn1_gu, ffn1_dn, l, 0, ns, ctx=ctx.reshape(1, seq, d))
        else:
            h = _ffn_half(h, mods, ffn1_norm, ffn1_gu, ffn1_dn, l, 0, ns)
        ya, qkv, z, cq, ckv, gate, ab, kr, krr = _mixer_in(h, mods, mix_norm, w_in_p, gm_ln, gm_ws_b, gm_bs_t, l)
        aux, aux_t = _delta_aux(ab, dn_a_log, dn_dt_bias, l)
        qkvn = _delta_conv(qkv, dn_conv, l, seq_ctx)
        o_x, o_c = _delta_rule(qkvn, aux, aux_t, batch, seq_ctx)
        qh, kh, vh = _mla_prep(cq, ckv, kr, krr, rope, mla_q_norm, w_uq_p, w_uq_rot, mla_kv_norm, w_uk_p, w_uv,
                               gq, gq_rot, gk, gk_rot, l)
        yc = _attn_latent(qh, kh, vh, batch, seq_ctx)
        n_out = ns if need_ctx else batch
        if need_ctx:
            yc = _attn_context(qh, kh, vh, yc, batch, seq_ctx)
        h = _merge(h, mods, ya, o_x, o_c, z, yc, gate, dn_out_norm, w_a, w_b, w_c, w_o, l, n_out)
        h = _ffn_half(h, mods, ffn2_norm, ffn2_gu, ffn2_dn, l, 6, n_out)
    return h[:batch]
```

```python
import functools
import math

import jax
import jax.numpy as jnp
from jax import lax
from jax.experimental import pallas as pl
from jax.experimental.pallas import tpu as pltpu

F32 = jnp.float32
BF16 = jnp.bfloat16

D_MODEL = 1024
DEPTH = 4
GRID_W = 64
CTX_LEN = 256
RMS_EPS = 1e-6
N_MOD = 9
FFN_HIDDEN = 2816

GM_GROUPS = 4
GM_WIDTH = 512
GM_CHUNK = 128

DN_HEADS = 4
DN_HEAD_DIM = 128
DN_WIDTH = DN_HEADS * DN_HEAD_DIM
DN_CONV = 5
DN_BLOCK = 128

MLA_HEADS = 8
MLA_NOPE = 64
MLA_ROPE = 32
MLA_QK = MLA_NOPE + MLA_ROPE
MLA_V = 64
MLA_WIDTH = MLA_HEADS * MLA_V
MLA_Q_LORA = 384
MLA_KV_LORA = 256
ROPE_BASE = 10000.0
N_BRANCH = 3

LANE = 128
HEAD_PAD = 128
MOD_ROWS = 16
AUX_W = 128
VMEM_LIMIT = 56 * 1024 * 1024

_LOG2E = 1.4426950408889634
FFN_CHUNK = 256
ROW_TILE = 512
MIX_TILE = 512


def _cparams(n_axes):
    return pltpu.CompilerParams(dimension_semantics=("arbitrary",) * n_axes,
                                vmem_limit_bytes=VMEM_LIMIT)


def _dot(a, b):
    return jnp.dot(a, b, preferred_element_type=F32)


def _dot_nt(a, b):
    return lax.dot_general(a, b, (((1,), (1,)), ((), ())), preferred_element_type=F32)


def _sigmoid(x):
    return 1.0 / (1.0 + jnp.exp(-x))


def _silu(x):
    return x * _sigmoid(x)


def _gelu_tanh(x):
    return 0.5 * x * (1.0 + jnp.tanh(math.sqrt(2.0 / math.pi) * (x + 0.044715 * (x * x * x))))


def _const_spec(shape, single=True):
    nd = len(shape)
    kw = {"pipeline_mode": pl.Buffered(1)} if single else {}
    return pl.BlockSpec(shape, lambda *_: (0,) * nd, **kw)


def _layer_spec(layer, shape):
    nd = len(shape)
    return pl.BlockSpec((None,) + tuple(shape), lambda *_: (layer,) + (0,) * nd,
                        pipeline_mode=pl.Buffered(1))


def _mod_spec(layer):
    return pl.BlockSpec((None, None, N_MOD, D_MODEL), lambda s, *_: (layer, s, 0, 0))


def _ada_kernel(x_ref, w_ref, b_ref, o_ref):
    x = x_ref[...]
    xs = _silu(x).astype(BF16)
    o_ref[...] = _dot(xs, w_ref[...].astype(BF16)) + b_ref[...]


def _ada_table(cond, ada_w, ada_b):
    nb = D_MODEL
    out = pl.pallas_call(
        _ada_kernel,
        grid=(DEPTH, N_MOD),
        in_specs=[
            pl.BlockSpec((MOD_ROWS, D_MODEL), lambda l, j: (0, 0)),
            pl.BlockSpec((None, D_MODEL, nb), lambda l, j: (l, 0, j)),
            pl.BlockSpec((None, 1, nb), lambda l, j: (l, 0, j)),
        ],
        out_specs=pl.BlockSpec((None, MOD_ROWS, nb), lambda l, j: (l, 0, j)),
        out_shape=jax.ShapeDtypeStruct((DEPTH, MOD_ROWS, N_MOD * D_MODEL), F32),
        compiler_params=_cparams(2),
        name="ada_table",
    )(cond, ada_w, ada_b.reshape(DEPTH, 1, N_MOD * D_MODEL))
    return out.reshape(DEPTH, MOD_ROWS, N_MOD, D_MODEL)


def _mod_rmsnorm(h, norm_w, shift, scale):
    ms = jnp.mean(h * h, axis=-1, keepdims=True)
    xn = h * lax.rsqrt(ms + RMS_EPS) * norm_w
    return xn * (1.0 + scale) + shift


def _ffn_kernel(*refs, k0, n_latent):
    if n_latent is None:
        h_ref, mod_ref, nw_ref, wgu_ref, wd_ref, o_ref = refs
        h = h_ref[...]
    else:
        x_ref, ctx_ref, mod_ref, nw_ref, wgu_ref, wd_ref, o_ref = refs
        h = jnp.where(pl.program_id(0) < n_latent, x_ref[...], ctx_ref[...])
    xm = _mod_rmsnorm(h, nw_ref[...], mod_ref[k0:k0 + 1, :], mod_ref[k0 + 1:k0 + 2, :]).astype(BF16)
    acc = jnp.zeros(h.shape, F32)
    for c in range(FFN_HIDDEN // FFN_CHUNK):
        lo = c * FFN_CHUNK
        g = _dot(xm, wgu_ref[:, lo:lo + FFN_CHUNK])
        u = _dot(xm, wgu_ref[:, FFN_HIDDEN + lo:FFN_HIDDEN + lo + FFN_CHUNK])
        a = (_silu(g) * u).astype(BF16)
        acc = acc + _dot(a, wd_ref[lo:lo + FFN_CHUNK, :])
    o_ref[...] = h + 0.5 * mod_ref[k0 + 2:k0 + 3, :] * acc


def _ffn_half(h, mods, norm_w, w_gu, w_down, layer, k0, n_samples, ctx=None):
    _, seq, d = h.shape
    tm = min(ROW_TILE, seq)
    tok = pl.BlockSpec((None, tm, d), lambda s, t: (s, t, 0))
    if ctx is None:
        stream, stream_specs, n_latent = (h,), [tok], None
    else:
        n_latent = h.shape[0]
        last_t = seq // tm - 1
        x_spec = pl.BlockSpec((None, tm, d),
                              lambda s, t: (jnp.minimum(s, n_latent - 1), jnp.where(s < n_latent, t, last_t), 0))
        c_spec = pl.BlockSpec((None, tm, d), lambda s, t: (0, jnp.where(s < n_latent, 0, t), 0))
        stream, stream_specs = (h, ctx), [x_spec, c_spec]
    return pl.pallas_call(
        functools.partial(_ffn_kernel, k0=k0, n_latent=n_latent),
        grid=(n_samples, seq // tm),
        in_specs=stream_specs + [_mod_spec(layer), _layer_spec(layer, (1, d)),
                                 _layer_spec(layer, (d, 2 * FFN_HIDDEN)), _layer_spec(layer, (FFN_HIDDEN, d))],
        out_specs=tok,
        out_shape=jax.ShapeDtypeStruct((n_samples, seq, d), F32),
        compiler_params=_cparams(2),
        name=f"ffn_half_k{k0}",
    )(*stream, mods, norm_w.reshape(DEPTH, 1, d), w_gu, w_down)


_C_GM = 0
_C_QKV = _C_GM + 2 * GM_WIDTH
_C_Z = _C_QKV + 3 * DN_WIDTH
_C_CQ = _C_Z + DN_WIDTH
_C_AB = _C_CQ + MLA_Q_LORA
_C_CKV = _C_AB + AUX_W
_C_GATE = _C_CKV + MLA_KV_LORA
_C_KR = _C_GATE + N_BRANCH * D_MODEL
_C_KRR = _C_KR + HEAD_PAD
_C_END = _C_KRR + HEAD_PAD


def _rot_half_cols(w):
    q4 = MLA_ROPE // 4
    a, b, c, e = (w[..., i * q4:(i + 1) * q4] for i in range(4))
    return jnp.concatenate([-b, a, -e, c], -1)


def _rot_half_gain(g):
    q4 = MLA_ROPE // 4
    a, b, c, e = (g[..., i * q4:(i + 1) * q4] for i in range(4))
    return jnp.concatenate([b, a, e, c], -1)


def _pack_w_in(w_in):
    splits = (2 * GM_WIDTH, 3 * DN_WIDTH, DN_WIDTH, 4 * DN_HEADS, MLA_Q_LORA, MLA_KV_LORA + MLA_ROPE,
              N_BRANCH * D_MODEL)
    offs = [0]
    for s in splits:
        offs.append(offs[-1] + s)
    gm, qkv, z, ab, cq, ckvr, gate = (w_in[..., offs[i]:offs[i + 1]] for i in range(7))
    ckv, kr = ckvr[..., :MLA_KV_LORA], ckvr[..., MLA_KV_LORA:]
    lead = w_in.shape[:-1]
    ab_p = jnp.concatenate([ab, jnp.zeros(lead + (AUX_W - 4 * DN_HEADS,), w_in.dtype)], -1)
    def rope_group(cols):
        return jnp.concatenate([jnp.zeros(lead + (MLA_NOPE,), w_in.dtype), cols,
                                jnp.zeros(lead + (HEAD_PAD - MLA_QK,), w_in.dtype)], -1)

    return jnp.concatenate([gm, qkv, z, cq, ab_p, ckv, gate, rope_group(kr), rope_group(_rot_half_cols(kr))],
                           -1).astype(BF16)


def _mixer_in_kernel(h_ref, mod_ref, nw_ref, w_ref, ln_ref, ws_ref, bs_ref,
                     ya_ref, qkv_ref, z_ref, cq_ref, ckv_ref, gate_ref, ab_ref, kr_ref, krr_ref):
    h = h_ref[...]
    um = _mod_rmsnorm(h, nw_ref[...], mod_ref[3:4, :], mod_ref[4:5, :]).astype(BF16)
    zz_pre = _dot(um, w_ref[:, _C_GM:_C_QKV])
    qkv_ref[...] = _dot(um, w_ref[:, _C_QKV:_C_Z]).astype(qkv_ref.dtype)
    z_ref[...] = _dot(um, w_ref[:, _C_Z:_C_CQ]).astype(z_ref.dtype)
    cq_ab = _dot(um, w_ref[:, _C_CQ:_C_CKV])
    cq_ref[...] = cq_ab[:, :MLA_Q_LORA]
    ab_ref[...] = cq_ab[:, MLA_Q_LORA:]
    ckv_ref[...] = _dot(um, w_ref[:, _C_CKV:_C_GATE])
    gate_ref[...] = _dot(um, w_ref[:, _C_GATE:_C_KR]).astype(gate_ref.dtype)
    kr_both = _dot(um, w_ref[:, _C_KR:_C_END])
    kr_ref[...] = kr_both[:, :HEAD_PAD]
    krr_ref[...] = kr_both[:, HEAD_PAD:]
    zz = _gelu_tanh(zz_pre)
    uu = zz[:, :GM_WIDTH]
    v = zz[:, GM_WIDTH:]
    mu = jnp.mean(v, axis=-1, keepdims=True)
    vc = v - mu
    var = jnp.mean(vc * vc, axis=-1, keepdims=True)
    vn = (vc * lax.rsqrt(var + RMS_EPS) * ln_ref[...]).astype(BF16)
    gw = GM_WIDTH // GM_GROUPS
    for c in range(h.shape[0] // GM_CHUNK):
        r = slice(c * GM_CHUNK, (c + 1) * GM_CHUNK)
        for g in range(GM_GROUPS):
            cs = slice(g * gw, (g + 1) * gw)
            s = _dot(ws_ref[g], vn[r, cs]) + bs_ref[:, g:g + 1]
            ya_ref[r, cs] = (uu[r, cs] * s).astype(ya_ref.dtype)


def _mixer_in(h, mods, norm_w, w_packed, gm_ln, gm_ws, gm_bs_t, layer):
    ns, seq, d = h.shape
    tm = min(MIX_TILE, seq)

    def tok(width):
        return pl.BlockSpec((None, tm, width), lambda s, t: (s, t, 0))

    def out(width, dtype=F32):
        return jax.ShapeDtypeStruct((ns, seq, width), dtype)

    return pl.pallas_call(
        _mixer_in_kernel,
        grid=(ns, seq // tm),
        in_specs=[tok(d), _mod_spec(layer), _layer_spec(layer, (1, d)), _layer_spec(layer, (d, _C_END)),
                  _layer_spec(layer, (1, GM_WIDTH)),
                  _layer_spec(layer, (GM_GROUPS, GM_CHUNK, GM_CHUNK)),
                  _layer_spec(layer, (GM_CHUNK, GM_GROUPS))],
        out_specs=[tok(GM_WIDTH), tok(3 * DN_WIDTH), tok(DN_WIDTH), tok(MLA_Q_LORA), tok(MLA_KV_LORA),
                   tok(N_BRANCH * d), tok(AUX_W), tok(HEAD_PAD), tok(HEAD_PAD)],
        out_shape=[out(GM_WIDTH, BF16), out(3 * DN_WIDTH, BF16), out(DN_WIDTH, BF16), out(MLA_Q_LORA),
                   out(MLA_KV_LORA), out(N_BRANCH * d, BF16), out(AUX_W), out(HEAD_PAD), out(HEAD_PAD)],
        compiler_params=_cparams(2),
        name="mixer_in",
    )(h, mods, norm_w.reshape(DEPTH, 1, d), w_packed, gm_ln.reshape(DEPTH, 1, GM_WIDTH), gm_ws, gm_bs_t)


def _tri(n, upper, inclusive=True):
    r = lax.broadcasted_iota(jnp.int32, (n, n), 0)
    c = lax.broadcasted_iota(jnp.int32, (n, n), 1)
    if upper:
        return (r <= c) if inclusive else (r < c)
    return (r >= c) if inclusive else (r > c)


def _delta_aux_kernel(ab_ref, alog_ref, dtb_ref, a_ref, at_ref):
    nh2 = 2 * DN_HEADS
    ab = ab_ref[...]
    sp = jnp.maximum(ab + dtb_ref[...], 0.0) + jnp.log1p(jnp.exp(-jnp.abs(ab + dtb_ref[...])))
    g = -jnp.exp(alog_ref[...]) * sp
    lane = lax.broadcasted_iota(jnp.int32, (DN_BLOCK, AUX_W), 1)
    tri_lo = _tri(DN_BLOCK, False).astype(F32)
    tri_up = _tri(DN_BLOCK, True).astype(F32)
    ones = jnp.ones((DN_BLOCK, DN_BLOCK), F32)
    beta = _sigmoid(ab)
    for c in range(ab.shape[0] // DN_BLOCK):
        r = slice(c * DN_BLOCK, (c + 1) * DN_BLOCK)
        gb = jnp.where(lane < nh2, g[r], 0.0)
        cf = jnp.dot(tri_lo, gb, preferred_element_type=F32, precision=lax.Precision.HIGHEST)
        cr = jnp.dot(tri_up, gb, preferred_element_type=F32, precision=lax.Precision.HIGHEST)
        tot = jnp.dot(ones, gb, preferred_element_type=F32, precision=lax.Precision.HIGHEST)
        gam = jnp.where(lane < DN_HEADS, cf, cr)
        tot_sh = pltpu.roll(tot, 2 * nh2, 1)
        blk = jnp.where(lane < nh2, gam, jnp.where(lane < 2 * nh2, beta[r], jnp.where(lane < 3 * nh2, tot_sh, 0.0)))
        a_ref[r, :] = blk
        at_ref[c] = blk.T[:4 * nh2, :]


def _delta_aux(ab, a_log, dt_bias, layer):
    ns, seq, _ = ab.shape
    nblk = seq // DN_BLOCK
    nh2 = 2 * DN_HEADS

    def row(p):
        return jnp.concatenate([p.reshape(DEPTH, 1, nh2), jnp.zeros((DEPTH, 1, AUX_W - nh2), F32)], -1)

    return pl.pallas_call(
        _delta_aux_kernel,
        grid=(ns,),
        in_specs=[pl.BlockSpec((None, seq, AUX_W), lambda s: (s, 0, 0)),
                  _layer_spec(layer, (1, AUX_W)), _layer_spec(layer, (1, AUX_W))],
        out_specs=[pl.BlockSpec((None, seq, AUX_W), lambda s: (s, 0, 0)),
                   pl.BlockSpec((None, nblk, 4 * nh2, DN_BLOCK), lambda s: (s, 0, 0, 0))],
        out_shape=[jax.ShapeDtypeStruct((ns, seq, AUX_W), F32),
                   jax.ShapeDtypeStruct((ns, nblk, 4 * nh2, DN_BLOCK), F32)],
        compiler_params=_cparams(1),
        name="delta_aux",
    )(ab, row(a_log), row(dt_bias))


_CONV_SUB = 256
_CONV_HALO = 8


def _delta_conv_kernel(x_ref, w_ref, o_ref, xp_ref, *, seq_ctx):
    s = pl.program_id(0)
    j = pl.program_id(1)
    seq = x_ref.shape[0]
    width = x_ref.shape[1]
    zeros = jnp.zeros((_CONV_HALO, width), F32)
    xp_ref[0:_CONV_HALO, :] = zeros
    xp_ref[_CONV_HALO + seq:, :] = zeros
    xp_ref[_CONV_HALO:_CONV_HALO + seq, :] = x_ref[...].astype(F32)
    is_ctx = s == pl.num_programs(0) - 1
    pad = DN_CONV // 2
    edge = _CONV_HALO
    assert seq_ctx == _CONV_SUB
    row_e = lax.broadcasted_iota(jnp.int32, (edge, 1), 0)

    def taps(start, nrows, ok_fn):
        acc = jnp.zeros((nrows, width), F32)
        for k in range(DN_CONV):
            xs = xp_ref[_CONV_HALO + start + k - pad:_CONV_HALO + start + k - pad + nrows, :]
            if ok_fn is not None:
                xs = jnp.where(ok_fn(k - pad), xs, 0.0)
            acc = acc + xs * w_ref[k:k + 1, :]
        return acc

    assert x_ref.dtype == BF16
    r_i = lax.broadcasted_iota(jnp.int32, (_CONV_SUB, _CONV_SUB), 0)
    c_i = lax.broadcasted_iota(jnp.int32, (_CONV_SUB, _CONV_SUB), 1)
    off_taps = [k for k in range(DN_CONV) if k != pad]
    shift_all = jnp.concatenate([jnp.where(c_i == r_i + (k - pad), 1.0, 0.0).astype(BF16) for k in off_taps], axis=0)

    def interior(r0):
        shifted = _dot(shift_all, x_ref[r0:r0 + _CONV_SUB, :])
        acc = xp_ref[_CONV_HALO + r0:_CONV_HALO + r0 + _CONV_SUB, :] * w_ref[pad:pad + 1, :]
        for n, k in enumerate(off_taps):
            acc = acc + shifted[n * _CONV_SUB:(n + 1) * _CONV_SUB] * w_ref[k:k + 1, :]
        return acc

    not_ctx = jnp.logical_not(is_ctx)
    for t in range(seq // _CONV_SUB):
        r0 = t * _CONV_SUB
        mid = interior(r0)
        top = taps(r0, edge, lambda sh: not_ctx | (row_e + sh >= 0))
        bot = taps(r0 + _CONV_SUB - edge, edge, lambda sh: not_ctx | (row_e + sh < edge))
        acc = jnp.concatenate([top, mid[edge:_CONV_SUB - edge], bot], axis=0)
        y = _silu(acc)
        for hd in range(width // DN_HEAD_DIM):
            cs = slice(hd * DN_HEAD_DIM, (hd + 1) * DN_HEAD_DIM)
            yh = y[:, cs]
            inv = lax.rsqrt(jnp.sum(yh * yh, axis=-1, keepdims=True) + RMS_EPS)
            fac = jnp.where(j == 0, inv * DN_HEAD_DIM ** -0.5, jnp.where(j == 1, inv, 1.0))
            o_ref[r0:r0 + _CONV_SUB, cs] = (yh * fac).astype(o_ref.dtype)


def _delta_conv(qkv, conv_w, layer, seq_ctx):
    ns, seq, _ = qkv.shape
    blk = pl.BlockSpec((None, seq, DN_WIDTH), lambda s, j: (s, 0, j))
    return pl.pallas_call(
        functools.partial(_delta_conv_kernel, seq_ctx=seq_ctx),
        grid=(ns, 3),
        in_specs=[blk, pl.BlockSpec((None, DN_CONV, DN_WIDTH), lambda s, j: (layer, 0, j))],
        out_specs=blk,
        out_shape=jax.ShapeDtypeStruct((ns, seq, 3 * DN_WIDTH), BF16),
        scratch_shapes=[pltpu.VMEM((seq + 2 * _CONV_HALO, DN_WIDTH), F32)],
        compiler_params=_cparams(2),
        name="delta_conv",
    )(qkv, conv_w)


def _delta_step(views, s_ref, i, nblk):
    nh2 = 2 * DN_HEADS
    n = DN_BLOCK
    ids = [(g, d, hd) for g in range(len(views)) for d in range(2) for hd in range(DN_HEADS)]
    bis = [i, nblk - 1 - i]
    ablk = [[vw[3](bi) for bi in bis] for vw in views]
    atblk = [[vw[4](bi) for bi in bis] for vw in views]

    r_i = lax.broadcasted_iota(jnp.int32, (n, n), 0)
    c_i = lax.broadcasted_iota(jnp.int32, (n, n), 1)
    x_i = r_i ^ c_i
    eye = (r_i == c_i).astype(F32)
    incl = [r_i >= c_i, r_i <= c_i]
    strict = [r_i > c_i, r_i < c_i]

    def col(g, d, hd, grp):
        c = grp * nh2 + d * DN_HEADS + hd
        return ablk[g][d][:, c:c + 1]

    def row(g, d, hd, grp):
        c = grp * nh2 + d * DN_HEADS + hd
        return atblk[g][d][c:c + 1, :]

    q = [views[g][0](bis[d], hd).astype(F32) for g, d, hd in ids]
    k = [views[g][1](bis[d], hd).astype(F32) for g, d, hd in ids]
    v = [views[g][2](bis[d], hd).astype(F32) for g, d, hd in ids]
    gam_c = [col(g, d, hd, 0) for g, d, hd in ids]
    beta_c = [col(g, d, hd, 1) for g, d, hd in ids]
    tot_c = [col(g, d, hd, 2) for g, d, hd in ids]
    decay = [jnp.where(incl[d], jnp.exp(jnp.where(incl[d], col(g, d, hd, 0) - row(g, d, hd, 0), 0.0)), 0.0)
             for g, d, hd in ids]
    kb = [kj * bj for kj, bj in zip(k, beta_c)]
    kf = [kj.astype(BF16) for kj in k]
    kk = [_dot_nt(kbj.astype(BF16), kfj) for kbj, kfj in zip(kb, kf)]
    qk = [_dot_nt(qj.astype(BF16), kfj) for qj, kfj in zip(q, kf)]
    m = [jnp.where(strict[d], kkj * dj, 0.0) for (_, d, _), kkj, dj in zip(ids, kk, decay)]
    qkd = [(qkj * dj).astype(BF16) for qkj, dj in zip(qk, decay)]

    t = [eye - jnp.where(x_i < 2, mj, 0.0) for mj in m]
    s = 2
    while s < n:
        sel = (x_i >= s) & (x_i < 2 * s)
        tb = [tj.astype(BF16) for tj in t]
        y = [_dot(jnp.where(sel, mj, 0.0).astype(BF16), tbj).astype(BF16) for mj, tbj in zip(m, tb)]
        z = [_dot(tbj, yj) for tbj, yj in zip(tb, y)]
        t = [tj - zj for tj, zj in zip(t, z)]
        s *= 2

    rhs = [jnp.concatenate([vj * bj, kbj * jnp.exp(gj)], axis=1).astype(BF16)
           for vj, bj, kbj, gj in zip(v, beta_c, kb, gam_c)]
    uw = [_dot(tj.astype(BF16), rj) for tj, rj in zip(t, rhs)]
    kd_t = [(kj * jnp.exp(tc - gj)).T.astype(BF16) for kj, tc, gj in zip(k, tot_c, gam_c)]

    s_prev = [s_ref[(g * 2 + d) * DN_HEADS + hd] for g, d, hd in ids]
    lhs = [jnp.concatenate([uwj[:, DN_HEAD_DIM:], qj * jnp.exp(gj)], axis=0).astype(BF16)
           for uwj, qj, gj in zip(uw, q, gam_c)]
    ws_qs = [_dot(lj, sj.astype(BF16)) for lj, sj in zip(lhs, s_prev)]
    vnb = [(uwj[:, :DN_HEAD_DIM] - wq[:n]).astype(BF16) for uwj, wq in zip(uw, ws_qs)]
    o = [wq[n:] + _dot(qkj, vj) for wq, qkj, vj in zip(ws_qs, qkd, vnb)]
    s_new = [sj * jnp.exp(row(g, d, hd, 2)) + _dot(kdj, vj)
             for (g, d, hd), sj, kdj, vj in zip(ids, s_prev, kd_t, vnb)]
    for (g, d, hd), oj, sj in zip(ids, o, s_new):
        s_ref[(g * 2 + d) * DN_HEADS + hd] = sj
        views[g][5](bis[d], hd, oj)


def _delta_kernel(qx_ref, kx_ref, vx_ref, ax_ref, atx_ref, qc_ref, kc_ref, vc_ref, ac_ref, atc_ref,
                  ox_ref, oc_ref, s_ref):
    s_ref[...] = jnp.zeros(s_ref.shape, F32)
    ox_ref[...] = jnp.zeros(ox_ref.shape, ox_ref.dtype)
    oc_ref[...] = jnp.zeros(oc_ref.shape, oc_ref.dtype)
    n = DN_BLOCK
    group, seq_ctx = oc_ref.shape[0], oc_ref.shape[1]

    def cols(hd):
        return slice(hd * DN_HEAD_DIM, (hd + 1) * DN_HEAD_DIM)

    def rows(bi, base=0):
        return pl.ds(pl.multiple_of(base + bi * n, n), n)

    def latent_view(g):
        def add(bi, hd, val):
            cur = ox_ref[g, rows(bi), cols(hd)].astype(F32)
            ox_ref[g, rows(bi), cols(hd)] = (cur + val).astype(ox_ref.dtype)
        return (lambda bi, hd: qx_ref[g, rows(bi), cols(hd)], lambda bi, hd: kx_ref[g, rows(bi), cols(hd)],
                lambda bi, hd: vx_ref[g, rows(bi), cols(hd)], lambda bi: ax_ref[g, rows(bi), :],
                lambda bi: atx_ref[g, bi], add)

    def context_view(g):
        base = g * seq_ctx

        def add(bi, hd, val):
            cur = oc_ref[g, rows(bi), cols(hd)].astype(F32)
            oc_ref[g, rows(bi), cols(hd)] = (cur + val).astype(oc_ref.dtype)
        return (lambda bi, hd: qc_ref[rows(bi, base), cols(hd)], lambda bi, hd: kc_ref[rows(bi, base), cols(hd)],
                lambda bi, hd: vc_ref[rows(bi, base), cols(hd)], lambda bi: ac_ref[rows(bi, base), :],
                lambda bi: atc_ref[base // n + bi], add)

    def scan(views, nblk):
        def body(i, carry):
            _delta_step(views, s_ref, i, nblk)
            return carry
        lax.fori_loop(0, nblk, body, 0)

    scan([context_view(g) for g in range(group)], seq_ctx // n)
    scan([latent_view(g) for g in range(group)], ox_ref.shape[1] // n)


DN_GROUP = 2


def _delta_rule(qkvn, aux, aux_t, batch, seq_ctx):
    ns, seq, _ = qkvn.shape
    nblk = seq // DN_BLOCK
    nh2 = 2 * DN_HEADS
    grp = DN_GROUP if batch % DN_GROUP == 0 else 1
    one = pl.Buffered(1)

    def lat(j):
        return pl.BlockSpec((grp, seq, DN_WIDTH), lambda b: (b, 0, j), pipeline_mode=one)

    def ctx(j):
        return pl.BlockSpec((None, grp * seq_ctx, DN_WIDTH), lambda b: (batch, b, j))

    return pl.pallas_call(
        _delta_kernel,
        grid=(batch // grp,),
        in_specs=[lat(0), lat(1), lat(2),
                  pl.BlockSpec((grp, seq, AUX_W), lambda b: (b, 0, 0), pipeline_mode=one),
                  pl.BlockSpec((grp, nblk, 4 * nh2, DN_BLOCK), lambda b: (b, 0, 0, 0), pipeline_mode=one),
                  ctx(0), ctx(1), ctx(2),
                  pl.BlockSpec((None, grp * seq_ctx, AUX_W), lambda b: (batch, b, 0)),
                  pl.BlockSpec((None, grp * seq_ctx // DN_BLOCK, 4 * nh2, DN_BLOCK), lambda b: (batch, b, 0, 0))],
        out_specs=[pl.BlockSpec((grp, seq, DN_WIDTH), lambda b: (b, 0, 0)),
                   pl.BlockSpec((grp, seq_ctx, DN_WIDTH), lambda b: (b, 0, 0))],
        out_shape=[jax.ShapeDtypeStruct((batch, seq, DN_WIDTH), F32),
                   jax.ShapeDtypeStruct((batch, seq_ctx, DN_WIDTH), F32)],
        scratch_shapes=[pltpu.VMEM((grp * nh2, DN_HEAD_DIM, DN_HEAD_DIM), F32)],
        compiler_params=_cparams(1),
        name="delta_rule",
    )(qkvn, qkvn, qkvn, aux, aux_t, qkvn, qkvn, qkvn, aux, aux_t)


def _rope_tables(seq):
    rows = seq // GRID_W
    row = jnp.repeat(jnp.arange(rows), GRID_W).astype(F32)
    col = jnp.tile(jnp.arange(GRID_W), rows).astype(F32)
    half = MLA_ROPE // 2
    inv_freq = 1.0 / (ROPE_BASE ** (jnp.arange(0, half, 2, dtype=F32) / half))
    ang_r = row[:, None] * inv_freq
    ang_c = col[:, None] * inv_freq
    ang = jnp.concatenate([ang_r, ang_r, ang_c, ang_c], axis=-1)

    def pad(t, fill):
        return jnp.concatenate([jnp.full((seq, MLA_NOPE), fill, F32), t,
                                jnp.full((seq, HEAD_PAD - MLA_QK), fill, F32)], -1)

    lat = jnp.stack([pad(jnp.cos(ang), 1.0), pad(jnp.sin(ang), 0.0)])
    ident = jnp.stack([jnp.ones((seq, HEAD_PAD), F32), jnp.zeros((seq, HEAD_PAD), F32)])
    return jnp.stack([lat, ident])


def _head_norm_rope(xh, xh_rot, gain_cos, gain_sin, scale):
    ms = jnp.sum(xh * xh, axis=-1, keepdims=True) * (1.0 / MLA_QK)
    return (xh * gain_cos + xh_rot * gain_sin) * (lax.rsqrt(ms + RMS_EPS) * scale)


def _mla_prep_kernel(cq_ref, ckv_ref, kr_ref, krr_ref, rope_ref, qn_ref, wuq_ref, wuqr_ref, kvn_ref, wuk_ref,
                     wuv_ref, gq_ref, gqr_ref, gk_ref, gkr_ref, q_ref, k_ref, v_ref):
    cos, sin = rope_ref[0], rope_ref[1]
    gq_cos, gq_sin = gq_ref[...] * cos, gqr_ref[...] * sin
    gk_cos, gk_sin = gk_ref[...] * cos, gkr_ref[...] * sin
    cq = cq_ref[...]
    cqn = (cq * lax.rsqrt(jnp.mean(cq * cq, axis=-1, keepdims=True) + RMS_EPS) * qn_ref[...]).astype(BF16)
    q = _dot(cqn, wuq_ref[...])
    q_rot = _dot(cqn, wuqr_ref[...])
    ckv = ckv_ref[...]
    ckvn = (ckv * lax.rsqrt(jnp.mean(ckv * ckv, axis=-1, keepdims=True) + RMS_EPS) * kvn_ref[...]).astype(BF16)
    kn = _dot(ckvn, wuk_ref[...])
    lane = lax.broadcasted_iota(jnp.int32, (1, MLA_HEADS * HEAD_PAD), 1)
    ones_pad = jnp.where(lane % HEAD_PAD >= MLA_V, 1.0, 0.0)
    v_ref[...] = (_dot(ckvn, wuv_ref[...]) + ones_pad).astype(v_ref.dtype)
    kr = kr_ref[...]
    kr_rot_sin = krr_ref[...] * gk_sin
    for hd in range(MLA_HEADS):
        cs = slice(hd * HEAD_PAD, (hd + 1) * HEAD_PAD)
        qh = _head_norm_rope(q[:, cs], q_rot[:, cs], gq_cos, gq_sin, MLA_QK ** -0.5 * _LOG2E)
        q_ref[:, cs] = qh.astype(q_ref.dtype)
        xk = kn[:, cs] + kr
        ms = jnp.sum(xk * xk, axis=-1, keepdims=True) * (1.0 / MLA_QK)
        k_ref[:, cs] = ((xk * gk_cos + kr_rot_sin) * lax.rsqrt(ms + RMS_EPS)).astype(k_ref.dtype)


def _pad_heads(w, valid, lo=0):
    lead = w.shape[:-1]
    w = w.reshape(lead + (MLA_HEADS, valid))
    parts = []
    if lo:
        parts.append(jnp.zeros(lead + (MLA_HEADS, lo), w.dtype))
    parts.append(w)
    parts.append(jnp.zeros(lead + (MLA_HEADS, HEAD_PAD - lo - valid), w.dtype))
    return jnp.concatenate(parts, -1).reshape(lead + (MLA_HEADS * HEAD_PAD,))


def _pad_gain(g):
    return jnp.concatenate([g, jnp.zeros(g.shape[:-1] + (HEAD_PAD - MLA_QK,), g.dtype)], -1)[:, None, :]


def _pad_rot_gain(g):
    lead = g.shape[:-1]
    return jnp.concatenate([jnp.zeros(lead + (MLA_NOPE,), g.dtype), _rot_half_gain(g[..., MLA_NOPE:]),
                            jnp.zeros(lead + (HEAD_PAD - MLA_QK,), g.dtype)], -1)[:, None, :]


def _mla_prep(cq, ckv, kr, krr, rope, q_norm, w_uq_p, w_uq_rot, kv_norm, w_uk_p, w_uv, gq, gq_rot, gk, gk_rot,
              layer):
    ns, seq, _ = cq.shape
    tm = min(ROW_TILE, seq)

    def tok(width):
        return pl.BlockSpec((None, tm, width), lambda s, t: (s, t, 0))

    wq = MLA_HEADS * HEAD_PAD
    gain = _layer_spec(layer, (1, HEAD_PAD))
    return pl.pallas_call(
        _mla_prep_kernel,
        grid=(ns, seq // tm),
        in_specs=[tok(MLA_Q_LORA), tok(MLA_KV_LORA), tok(HEAD_PAD), tok(HEAD_PAD),
                  pl.BlockSpec((None, 2, tm, HEAD_PAD), lambda s, t: (s // (ns - 1), 0, t, 0)),
                  _layer_spec(layer, (1, MLA_Q_LORA)), _layer_spec(layer, (MLA_Q_LORA, wq)),
                  _layer_spec(layer, (MLA_Q_LORA, wq)),
                  _layer_spec(layer, (1, MLA_KV_LORA)), _layer_spec(layer, (MLA_KV_LORA, wq)),
                  _layer_spec(layer, (MLA_KV_LORA, wq)),
                  gain, gain, gain, gain],
        out_specs=[tok(wq), tok(wq), tok(wq)],
        out_shape=[jax.ShapeDtypeStruct((ns, seq, wq), BF16), jax.ShapeDtypeStruct((ns, seq, wq), BF16),
                   jax.ShapeDtypeStruct((ns, seq, wq), BF16)],
        compiler_params=_cparams(2),
        name="mla_prep",
    )(cq, ckv, kr, krr, rope, q_norm.reshape(DEPTH, 1, MLA_Q_LORA), w_uq_p, w_uq_rot,
      kv_norm.reshape(DEPTH, 1, MLA_KV_LORA), w_uk_p, w_uv, gq, gq_rot, gk, gk_rot)


def _attn_kernel(*refs, n_src):
    q_ref = refs[0]
    k_refs = refs[1:1 + n_src]
    v_refs = refs[1 + n_src:1 + 2 * n_src]
    o_ref = refs[1 + 2 * n_src]
    n_heads = q_ref.shape[1] // HEAD_PAD

    def cols(hh):
        return slice(hh * HEAD_PAD, (hh + 1) * HEAD_PAD)

    def score(hh):
        return [_dot_nt(q_ref[:, cols(hh)], k_ref[:, cols(hh)]) for k_ref in k_refs]

    def probs(scores):
        m = scores[0].max(axis=-1, keepdims=True)
        for sc in scores[1:]:
            m = jnp.maximum(m, sc.max(axis=-1, keepdims=True))
        return [jnp.exp2(sc - m).astype(BF16) for sc in scores]

    def weighted(hh, ps):
        acc = _dot(ps[0], v_refs[0][:, cols(hh)])
        for p, v_ref in zip(ps[1:], v_refs[1:]):
            acc = acc + _dot(p, v_ref[:, cols(hh)])
        return acc / acc[:, MLA_V:MLA_V + 1]

    ahead = 2
    scores = {hh: score(hh) for hh in range(min(ahead, n_heads))}
    heads = []
    for hh in range(n_heads):
        ps = probs(scores.pop(hh))
        if hh + ahead < n_heads:
            scores[hh + ahead] = score(hh + ahead)
        heads.append(weighted(hh, ps))
    lane = lax.broadcasted_iota(jnp.int32, (q_ref.shape[0], HEAD_PAD), 1)
    for pr in range(n_heads // 2):
        out = jnp.where(lane < MLA_V, heads[2 * pr], pltpu.roll(heads[2 * pr + 1], MLA_V, 1))
        o_ref[:, pr * 2 * MLA_V:(pr + 1) * 2 * MLA_V] = out.astype(o_ref.dtype)


ATTN_Q_TILE = 512
ATTN_HEADS_PER_STEP = 4


def _attn_latent(q, k, v, batch, seq_ctx):
    ns, seq, _ = q.shape
    tq = min(ATTN_Q_TILE, seq)
    pair = ATTN_HEADS_PER_STEP * HEAD_PAD
    return pl.pallas_call(
        functools.partial(_attn_kernel, n_src=2),
        grid=(batch, MLA_HEADS // ATTN_HEADS_PER_STEP, seq // tq),
        in_specs=[pl.BlockSpec((None, tq, pair), lambda b, h, t: (b, t, h)),
                  pl.BlockSpec((None, seq, pair), lambda b, h, t: (b, 0, h)),
                  pl.BlockSpec((None, seq_ctx, pair), lambda b, h, t: (batch, b, h)),
                  pl.BlockSpec((None, seq, pair), lambda b, h, t: (b, 0, h)),
                  pl.BlockSpec((None, seq_ctx, pair), lambda b, h, t: (batch, b, h))],
        out_specs=pl.BlockSpec((None, tq, ATTN_HEADS_PER_STEP * MLA_V), lambda b, h, t: (b, t, h)),
        out_shape=jax.ShapeDtypeStruct((ns, seq, MLA_WIDTH), BF16),
        compiler_params=_cparams(3),
        name="attn_latent",
    )(q, k, k, v, v)


def _attn_context(q, k, v, yc, batch, seq_ctx):
    ns, seq, _ = q.shape
    pair = ATTN_HEADS_PER_STEP * HEAD_PAD

    def kern(q_ref, k_ref, v_ref, yc_in_ref, o_ref):
        del yc_in_ref
        _attn_kernel(q_ref, k_ref, v_ref, o_ref, n_src=1)

    return pl.pallas_call(
        kern,
        grid=(batch, MLA_HEADS // ATTN_HEADS_PER_STEP),
        in_specs=[pl.BlockSpec((None, seq_ctx, pair), lambda b, h: (batch, b, h)),
                  pl.BlockSpec((None, seq_ctx, pair), lambda b, h: (batch, b, h)),
                  pl.BlockSpec((None, seq_ctx, pair), lambda b, h: (batch, b, h)),
                  pl.BlockSpec(memory_space=pl.ANY)],
        out_specs=pl.BlockSpec((None, seq_ctx, ATTN_HEADS_PER_STEP * MLA_V), lambda b, h: (batch, b, h)),
        out_shape=jax.ShapeDtypeStruct(yc.shape, yc.dtype),
        input_output_aliases={3: 0},
        compiler_params=_cparams(2),
        name="attn_context",
    )(q, k, v, yc)


def _merge_kernel(h_ref, mod_ref, ya_ref, ox_ref, oc_ref, z_ref, yc_ref, gate_ref, onw_ref,
                  wa_ref, wb_ref, wc_ref, wo_ref, out_ref, *, n_latent):
    d = h_ref.shape[1]
    is_latent = pl.program_id(0) < n_latent

    def gated_sum(rs):
        o = jnp.where(is_latent, ox_ref[rs, :], oc_ref[rs, :]).astype(F32)
        z = z_ref[rs, :].astype(F32)
        yb_parts = []
        for hd in range(DN_HEADS):
            cs = slice(hd * DN_HEAD_DIM, (hd + 1) * DN_HEAD_DIM)
            oh = o[:, cs]
            ohn = oh * lax.rsqrt(jnp.mean(oh * oh, axis=-1, keepdims=True) + RMS_EPS) * onw_ref[...]
            yb_parts.append(ohn * _silu(z[:, cs]))
        yb = jnp.concatenate(yb_parts, axis=1).astype(BF16)
        m = (_sigmoid(gate_ref[rs, 0:d].astype(F32)) * _dot(ya_ref[rs, :], wa_ref[...])
             + _sigmoid(gate_ref[rs, d:2 * d].astype(F32)) * _dot(yb, wb_ref[...])
             + _sigmoid(gate_ref[rs, 2 * d:3 * d].astype(F32)) * _dot(yc_ref[rs, :], wc_ref[...]))
        return m.astype(BF16)

    half = h_ref.shape[0] // 2
    parts = [slice(0, half), slice(half, 2 * half)]
    ms = [gated_sum(rs) for rs in parts]
    for rs, m in zip(parts, ms):
        out_ref[rs, :] = h_ref[rs, :] + mod_ref[5:6, :] * _dot(m, wo_ref[...])


def _merge(h, mods, ya, o_x, o_c, z, yc, gate, out_norm, w_a, w_b, w_c, w_o, layer, n_samples):
    ns, seq, d = h.shape
    tm = min(ROW_TILE, seq)

    def tok(width):
        return pl.BlockSpec((None, tm, width), lambda s, t: (s, t, 0))

    batch = o_x.shape[0]
    last_t = seq // tm - 1
    ox_spec = pl.BlockSpec((None, tm, DN_WIDTH),
                           lambda s, t: (jnp.minimum(s, batch - 1), jnp.where(s < batch, t, last_t), 0))
    oc_spec = pl.BlockSpec((None, tm, DN_WIDTH), lambda s, t: (0, jnp.where(s < batch, 0, t), 0))
    return pl.pallas_call(
        functools.partial(_merge_kernel, n_latent=batch),
        grid=(n_samples, seq // tm),
        in_specs=[tok(d), _mod_spec(layer), tok(GM_WIDTH), ox_spec, oc_spec, tok(DN_WIDTH), tok(MLA_WIDTH),
                  tok(N_BRANCH * d), _layer_spec(layer, (1, DN_HEAD_DIM)),
                  _layer_spec(layer, (GM_WIDTH, d)), _layer_spec(layer, (DN_WIDTH, d)),
                  _layer_spec(layer, (MLA_WIDTH, d)), _layer_spec(layer, (d, d))],
        out_specs=tok(d),
        out_shape=jax.ShapeDtypeStruct((n_samples, seq, d), F32),
        compiler_params=_cparams(2),
        name="merge",
    )(h, mods, ya, o_x, o_c.reshape(1, seq, DN_WIDTH), z, yc, gate, out_norm.reshape(DEPTH, 1, DN_HEAD_DIM),
      w_a, w_b, w_c, w_o)


def kernel(x, c, ctx, c_ctx, ada_w, ada_b, ffn1_norm, ffn1_w_gu, ffn1_w_down, mix_norm, w_in, gm_ln, gm_ws, gm_bs, dn_conv, dn_a_log, dn_dt_bias, dn_out_norm, mla_q_norm, mla_w_uq, mla_kv_norm, mla_w_ukv, mla_qk_norm_q, mla_qk_norm_k, w_branch_gm, w_branch_dn, w_branch_mla, w_out, ffn2_norm, ffn2_w_gu, ffn2_w_down):
    batch, seq, d = x.shape
    seq_ctx = ctx.shape[1]
    assert d == D_MODEL and batch * seq_ctx == seq and batch + 1 <= MOD_ROWS
    assert seq % ROW_TILE == 0 or seq < ROW_TILE
    ns = batch + 1

    cond = jnp.concatenate([c, c_ctx[None, :], jnp.zeros((MOD_ROWS - ns, d), F32)], 0)
    mods = _ada_table(cond, ada_w, ada_b)
    h = None

    bf = lambda w: w.astype(BF16)
    ffn1_gu, ffn1_dn, ffn2_gu, ffn2_dn = bf(ffn1_w_gu), bf(ffn1_w_down), bf(ffn2_w_gu), bf(ffn2_w_down)
    w_in_p = _pack_w_in(w_in)
    gm_ws_b = bf(gm_ws)
    gm_bs_t = jnp.swapaxes(gm_bs, 1, 2)
    w_uq_p = bf(_pad_heads(mla_w_uq, MLA_QK))
    ukv = mla_w_ukv.reshape(DEPTH, MLA_KV_LORA, MLA_HEADS, MLA_NOPE + MLA_V)
    w_uk_p = bf(_pad_heads(ukv[..., :MLA_NOPE].reshape(DEPTH, MLA_KV_LORA, MLA_HEADS * MLA_NOPE), MLA_NOPE))
    w_uv = bf(_pad_heads(ukv[..., MLA_NOPE:].reshape(DEPTH, MLA_KV_LORA, MLA_WIDTH), MLA_V))
    uq_rope = mla_w_uq.reshape(DEPTH, MLA_Q_LORA, MLA_HEADS, MLA_QK)[..., MLA_NOPE:]
    w_uq_rot = bf(_pad_heads(_rot_half_cols(uq_rope).reshape(DEPTH, MLA_Q_LORA, MLA_HEADS * MLA_ROPE),
                             MLA_ROPE, lo=MLA_NOPE))
    gq, gk = _pad_gain(mla_qk_norm_q), _pad_gain(mla_qk_norm_k)
    gq_rot, gk_rot = _pad_rot_gain(mla_qk_norm_q), _pad_rot_gain(mla_qk_norm_k)
    w_a, w_b, w_c, w_o = bf(w_branch_gm), bf(w_branch_dn), bf(w_branch_mla), bf(w_out)
    rope = _rope_tables(seq)

    for l in range(DEPTH):
        need_ctx = l < DEPTH - 1
        if l == 0:
            h = _ffn_half(x, mods, ffn1_norm, ffn1_gu, ffn1_dn, l, 0, ns, ctx=ctx.reshape(1, seq, d))
        else:
            h = _ffn_half(h, mods, ffn1_norm, ffn1_gu, ffn1_dn, l, 0, ns)
        ya, qkv, z, cq, ckv, gate, ab, kr, krr = _mixer_in(h, mods, mix_norm, w_in_p, gm_ln, gm_ws_b, gm_bs_t, l)
        aux, aux_t = _delta_aux(ab, dn_a_log, dn_dt_bias, l)
        qkvn = _delta_conv(qkv, dn_conv, l, seq_ctx)
        o_x, o_c = _delta_rule(qkvn, aux, aux_t, batch, seq_ctx)
        qh, kh, vh = _mla_prep(cq, ckv, kr, krr, rope, mla_q_norm, w_uq_p, w_uq_rot, mla_kv_norm, w_uk_p, w_uv,
                               gq, gq_rot, gk, gk_rot, l)
        yc = _attn_latent(qh, kh, vh, batch, seq_ctx)
        n_out = ns if need_ctx else batch
        if need_ctx:
            yc = _attn_context(qh, kh, vh, yc, batch, seq_ctx)
        h = _merge(h, mods, ya, o_x, o_c, z, yc, gate, dn_out_norm, w_a, w_b, w_c, w_o, l, n_out)
        h = _ffn_half(h, mods, ffn2_norm, ffn2_gu, ffn2_dn, l, 6, n_out)
    return h[:batch]
```

```python
import functools
import math

import jax
import jax.numpy as jnp
from jax import lax
from jax.experimental import pallas as pl
from jax.experimental.pallas import tpu as pltpu

F32 = jnp.float32
BF16 = jnp.bfloat16

D_MODEL = 1024
DEPTH = 4
GRID_W = 64
CTX_LEN = 256
RMS_EPS = 1e-6
N_MOD = 9
FFN_HIDDEN = 2816

GM_GROUPS = 4
GM_WIDTH = 512
GM_CHUNK = 128

DN_HEADS = 4
DN_HEAD_DIM = 128
DN_WIDTH = DN_HEADS * DN_HEAD_DIM
DN_CONV = 5
DN_BLOCK = 128

MLA_HEADS = 8
MLA_NOPE = 64
MLA_ROPE = 32
MLA_QK = MLA_NOPE + MLA_ROPE
MLA_V = 64
MLA_WIDTH = MLA_HEADS * MLA_V
MLA_Q_LORA = 384
MLA_KV_LORA = 256
ROPE_BASE = 10000.0
N_BRANCH = 3

LANE = 128
HEAD_PAD = 128
MOD_ROWS = 16
AUX_W = 128
VMEM_LIMIT = 56 * 1024 * 1024

_LOG2E = 1.4426950408889634
FFN_CHUNK = 256
ROW_TILE = 512
MIX_TILE = 512


def _cparams(n_axes):
    return pltpu.CompilerParams(dimension_semantics=("arbitrary",) * n_axes,
                                vmem_limit_bytes=VMEM_LIMIT)


def _dot(a, b):
    return jnp.dot(a, b, preferred_element_type=F32)


def _dot_nt(a, b):
    return lax.dot_general(a, b, (((1,), (1,)), ((), ())), preferred_element_type=F32)


def _sigmoid(x):
    return 1.0 / (1.0 + jnp.exp(-x))


def _silu(x):
    return x * _sigmoid(x)


def _gelu_tanh(x):
    return 0.5 * x * (1.0 + jnp.tanh(math.sqrt(2.0 / math.pi) * (x + 0.044715 * (x * x * x))))


def _const_spec(shape, single=True):
    nd = len(shape)
    kw = {"pipeline_mode": pl.Buffered(1)} if single else {}
    return pl.BlockSpec(shape, lambda *_: (0,) * nd, **kw)


def _layer_spec(layer, shape):
    nd = len(shape)
    return pl.BlockSpec((None,) + tuple(shape), lambda *_: (layer,) + (0,) * nd,
                        pipeline_mode=pl.Buffered(1))


def _mod_spec(layer):
    return pl.BlockSpec((None, None, N_MOD, D_MODEL), lambda s, *_: (layer, s, 0, 0))


def _ada_kernel(x_ref, w_ref, b_ref, o_ref):
    x = x_ref[...]
    xs = _silu(x).astype(BF16)
    o_ref[...] = _dot(xs, w_ref[...].astype(BF16)) + b_ref[...]


def _ada_table(cond, ada_w, ada_b):
    nb = D_MODEL
    out = pl.pallas_call(
        _ada_kernel,
        grid=(DEPTH, N_MOD),
        in_specs=[
            pl.BlockSpec((MOD_ROWS, D_MODEL), lambda l, j: (0, 0)),
            pl.BlockSpec((None, D_MODEL, nb), lambda l, j: (l, 0, j)),
            pl.BlockSpec((None, 1, nb), lambda l, j: (l, 0, j)),
        ],
        out_specs=pl.BlockSpec((None, MOD_ROWS, nb), lambda l, j: (l, 0, j)),
        out_shape=jax.ShapeDtypeStruct((DEPTH, MOD_ROWS, N_MOD * D_MODEL), F32),
        compiler_params=_cparams(2),
        name="ada_table",
    )(cond, ada_w, ada_b.reshape(DEPTH, 1, N_MOD * D_MODEL))
    return out.reshape(DEPTH, MOD_ROWS, N_MOD, D_MODEL)


def _mod_rmsnorm(h, norm_w, shift, scale):
    ms = jnp.mean(h * h, axis=-1, keepdims=True)
    xn = h * lax.rsqrt(ms + RMS_EPS) * norm_w
    return xn * (1.0 + scale) + shift


def _ffn_kernel(*refs, k0, n_latent):
    if n_latent is None:
        h_ref, mod_ref, nw_ref, wgu_ref, wd_ref, o_ref = refs
        h = h_ref[...]
    else:
        x_ref, ctx_ref, mod_ref, nw_ref, wgu_ref, wd_ref, o_ref = refs
        h = jnp.where(pl.program_id(0) < n_latent, x_ref[...], ctx_ref[...])
    xm = _mod_rmsnorm(h, nw_ref[...], mod_ref[k0:k0 + 1, :], mod_ref[k0 + 1:k0 + 2, :]).astype(BF16)
    acc = jnp.zeros(h.shape, F32)
    for c in range(FFN_HIDDEN // FFN_CHUNK):
        lo = c * FFN_CHUNK
        g = _dot(xm, wgu_ref[:, lo:lo + FFN_CHUNK])
        u = _dot(xm, wgu_ref[:, FFN_HIDDEN + lo:FFN_HIDDEN + lo + FFN_CHUNK])
        a = (_silu(g) * u).astype(BF16)
        acc = acc + _dot(a, wd_ref[lo:lo + FFN_CHUNK, :])
    o_ref[...] = h + 0.5 * mod_ref[k0 + 2:k0 + 3, :] * acc


def _ffn_half(h, mods, norm_w, w_gu, w_down, layer, k0, n_samples, ctx=None):
    _, seq, d = h.shape
    tm = min(ROW_TILE, seq)
    tok = pl.BlockSpec((None, tm, d), lambda s, t: (s, t, 0))
    if ctx is None:
        stream, stream_specs, n_latent = (h,), [tok], None
    else:
        n_latent = h.shape[0]
        last_t = seq // tm - 1
        x_spec = pl.BlockSpec((None, tm, d),
                              lambda s, t: (jnp.minimum(s, n_latent - 1), jnp.where(s < n_latent, t, last_t), 0))
        c_spec = pl.BlockSpec((None, tm, d), lambda s, t: (0, jnp.where(s < n_latent, 0, t), 0))
        stream, stream_specs = (h, ctx), [x_spec, c_spec]
    return pl.pallas_call(
        functools.partial(_ffn_kernel, k0=k0, n_latent=n_latent),
        grid=(n_samples, seq // tm),
        in_specs=stream_specs + [_mod_spec(layer), _layer_spec(layer, (1, d)),
                                 _layer_spec(layer, (d, 2 * FFN_HIDDEN)), _layer_spec(layer, (FFN_HIDDEN, d))],
        out_specs=tok,
        out_shape=jax.ShapeDtypeStruct((n_samples, seq, d), F32),
        compiler_params=_cparams(2),
        name=f"ffn_half_k{k0}",
    )(*stream, mods, norm_w.reshape(DEPTH, 1, d), w_gu, w_down)


_C_GM = 0
_C_QKV = _C_GM + 2 * GM_WIDTH
_C_Z = _C_QKV + 3 * DN_WIDTH
_C_CQ = _C_Z + DN_WIDTH
_C_AB = _C_CQ + MLA_Q_LORA
_C_CKV = _C_AB + AUX_W
_C_GATE = _C_CKV + MLA_KV_LORA
_C_KR = _C_GATE + N_BRANCH * D_MODEL
_C_KRR = _C_KR + HEAD_PAD
_C_END = _C_KRR + HEAD_PAD


def _rot_half_cols(w):
    q4 = MLA_ROPE // 4
    a, b, c, e = (w[..., i * q4:(i + 1) * q4] for i in range(4))
    return jnp.concatenate([-b, a, -e, c], -1)


def _rot_half_gain(g):
    q4 = MLA_ROPE // 4
    a, b, c, e = (g[..., i * q4:(i + 1) * q4] for i in range(4))
    return jnp.concatenate([b, a, e, c], -1)


def _pack_w_in(w_in):
    splits = (2 * GM_WIDTH, 3 * DN_WIDTH, DN_WIDTH, 4 * DN_HEADS, MLA_Q_LORA, MLA_KV_LORA + MLA_ROPE,
              N_BRANCH * D_MODEL)
    offs = [0]
    for s in splits:
        offs.append(offs[-1] + s)
    gm, qkv, z, ab, cq, ckvr, gate = (w_in[..., offs[i]:offs[i + 1]] for i in range(7))
    ckv, kr = ckvr[..., :MLA_KV_LORA], ckvr[..., MLA_KV_LORA:]
    lead = w_in.shape[:-1]
    ab_p = jnp.concatenate([ab, jnp.zeros(lead + (AUX_W - 4 * DN_HEADS,), w_in.dtype)], -1)
    def rope_group(cols):
        return jnp.concatenate([jnp.zeros(lead + (MLA_NOPE,), w_in.dtype), cols,
                                jnp.zeros(lead + (HEAD_PAD - MLA_QK,), w_in.dtype)], -1)

    return jnp.concatenate([gm, qkv, z, cq, ab_p, ckv, gate, rope_group(kr), rope_group(_rot_half_cols(kr))],
                           -1).astype(BF16)


def _mixer_in_kernel(h_ref, mod_ref, nw_ref, w_ref, ln_ref, ws_ref, bs_ref,
                     ya_ref, qkv_ref, z_ref, cq_ref, ckv_ref, gate_ref, ab_ref, kr_ref, krr_ref):
    h = h_ref[...]
    um = _mod_rmsnorm(h, nw_ref[...], mod_ref[3:4, :], mod_ref[4:5, :]).astype(BF16)
    zz_pre = _dot(um, w_ref[:, _C_GM:_C_QKV])
    qkv_ref[...] = _dot(um, w_ref[:, _C_QKV:_C_Z]).astype(qkv_ref.dtype)
    z_ref[...] = _dot(um, w_ref[:, _C_Z:_C_CQ]).astype(z_ref.dtype)
    cq_ab = _dot(um, w_ref[:, _C_CQ:_C_CKV])
    cq_ref[...] = cq_ab[:, :MLA_Q_LORA]
    ab_ref[...] = cq_ab[:, MLA_Q_LORA:]
    ckv_ref[...] = _dot(um, w_ref[:, _C_CKV:_C_GATE])
    gate_ref[...] = _dot(um, w_ref[:, _C_GATE:_C_KR]).astype(gate_ref.dtype)
    kr_both = _dot(um, w_ref[:, _C_KR:_C_END])
    kr_ref[...] = kr_both[:, :HEAD_PAD]
    krr_ref[...] = kr_both[:, HEAD_PAD:]
    zz = _gelu_tanh(zz_pre)
    uu = zz[:, :GM_WIDTH]
    v = zz[:, GM_WIDTH:]
    mu = jnp.mean(v, axis=-1, keepdims=True)
    vc = v - mu
    var = jnp.mean(vc * vc, axis=-1, keepdims=True)
    vn = (vc * lax.rsqrt(var + RMS_EPS) * ln_ref[...]).astype(BF16)
    gw = GM_WIDTH // GM_GROUPS
    for c in range(h.shape[0] // GM_CHUNK):
        r = slice(c * GM_CHUNK, (c + 1) * GM_CHUNK)
        for g in range(GM_GROUPS):
            cs = slice(g * gw, (g + 1) * gw)
            s = _dot(ws_ref[g], vn[r, cs]) + bs_ref[:, g:g + 1]
            ya_ref[r, cs] = (uu[r, cs] * s).astype(ya_ref.dtype)


def _mixer_in(h, mods, norm_w, w_packed, gm_ln, gm_ws, gm_bs_t, layer):
    ns, seq, d = h.shape
    tm = min(MIX_TILE, seq)

    def tok(width):
        return pl.BlockSpec((None, tm, width), lambda s, t: (s, t, 0))

    def out(width, dtype=F32):
        return jax.ShapeDtypeStruct((ns, seq, width), dtype)

    return pl.pallas_call(
        _mixer_in_kernel,
        grid=(ns, seq // tm),
        in_specs=[tok(d), _mod_spec(layer), _layer_spec(layer, (1, d)), _layer_spec(layer, (d, _C_END)),
                  _layer_spec(layer, (1, GM_WIDTH)),
                  _layer_spec(layer, (GM_GROUPS, GM_CHUNK, GM_CHUNK)),
                  _layer_spec(layer, (GM_CHUNK, GM_GROUPS))],
        out_specs=[tok(GM_WIDTH), tok(3 * DN_WIDTH), tok(DN_WIDTH), tok(MLA_Q_LORA), tok(MLA_KV_LORA),
                   tok(N_BRANCH * d), tok(AUX_W), tok(HEAD_PAD), tok(HEAD_PAD)],
        out_shape=[out(GM_WIDTH, BF16), out(3 * DN_WIDTH, BF16), out(DN_WIDTH, BF16), out(MLA_Q_LORA),
                   out(MLA_KV_LORA), out(N_BRANCH * d, BF16), out(AUX_W), out(HEAD_PAD), out(HEAD_PAD)],
        compiler_params=_cparams(2),
        name="mixer_in",
    )(h, mods, norm_w.reshape(DEPTH, 1, d), w_packed, gm_ln.reshape(DEPTH, 1, GM_WIDTH), gm_ws, gm_bs_t)


def _tri(n, upper, inclusive=True):
    r = lax.broadcasted_iota(jnp.int32, (n, n), 0)
    c = lax.broadcasted_iota(jnp.int32, (n, n), 1)
    if upper:
        return (r <= c) if inclusive else (r < c)
    return (r >= c) if inclusive else (r > c)


def _delta_aux_kernel(ab_ref, alog_ref, dtb_ref, a_ref, at_ref):
    nh2 = 2 * DN_HEADS
    ab = ab_ref[...]
    sp = jnp.maximum(ab + dtb_ref[...], 0.0) + jnp.log1p(jnp.exp(-jnp.abs(ab + dtb_ref[...])))
    g = -jnp.exp(alog_ref[...]) * sp
    lane = lax.broadcasted_iota(jnp.int32, (DN_BLOCK, AUX_W), 1)
    tri_lo = _tri(DN_BLOCK, False).astype(F32)
    tri_up = _tri(DN_BLOCK, True).astype(F32)
    ones = jnp.ones((DN_BLOCK, DN_BLOCK), F32)
    beta = _sigmoid(ab)
    for c in range(ab.shape[0] // DN_BLOCK):
        r = slice(c * DN_BLOCK, (c + 1) * DN_BLOCK)
        gb = jnp.where(lane < nh2, g[r], 0.0)
        cf = jnp.dot(tri_lo, gb, preferred_element_type=F32, precision=lax.Precision.HIGHEST)
        cr = jnp.dot(tri_up, gb, preferred_element_type=F32, precision=lax.Precision.HIGHEST)
        tot = jnp.dot(ones, gb, preferred_element_type=F32, precision=lax.Precision.HIGHEST)
        gam = jnp.where(lane < DN_HEADS, cf, cr)
        tot_sh = pltpu.roll(tot, 2 * nh2, 1)
        blk = jnp.where(lane < nh2, gam, jnp.where(lane < 2 * nh2, beta[r], jnp.where(lane < 3 * nh2, tot_sh, 0.0)))
        a_ref[r, :] = blk
        at_ref[c] = blk.T[:4 * nh2, :]


def _delta_aux(ab, a_log, dt_bias, layer):
    ns, seq, _ = ab.shape
    nblk = seq // DN_BLOCK
    nh2 = 2 * DN_HEADS

    def row(p):
        return jnp.concatenate([p.reshape(DEPTH, 1, nh2), jnp.zeros((DEPTH, 1, AUX_W - nh2), F32)], -1)

    return pl.pallas_call(
        _delta_aux_kernel,
        grid=(ns,),
        in_specs=[pl.BlockSpec((None, seq, AUX_W), lambda s: (s, 0, 0)),
                  _layer_spec(layer, (1, AUX_W)), _layer_spec(layer, (1, AUX_W))],
        out_specs=[pl.BlockSpec((None, seq, AUX_W), lambda s: (s, 0, 0)),
                   pl.BlockSpec((None, nblk, 4 * nh2, DN_BLOCK), lambda s: (s, 0, 0, 0))],
        out_shape=[jax.ShapeDtypeStruct((ns, seq, AUX_W), F32),
                   jax.ShapeDtypeStruct((ns, nblk, 4 * nh2, DN_BLOCK), F32)],
        compiler_params=_cparams(1),
        name="delta_aux",
    )(ab, row(a_log), row(dt_bias))


_CONV_SUB = 256
_CONV_HALO = 8


def _delta_conv_kernel(x_ref, w_ref, o_ref, xp_ref, *, seq_ctx):
    s = pl.program_id(0)
    j = pl.program_id(1)
    seq = x_ref.shape[0]
    width = x_ref.shape[1]
    zeros = jnp.zeros((_CONV_HALO, width), F32)
    xp_ref[0:_CONV_HALO, :] = zeros
    xp_ref[_CONV_HALO + seq:, :] = zeros
    xp_ref[_CONV_HALO:_CONV_HALO + seq, :] = x_ref[...].astype(F32)
    is_ctx = s == pl.num_programs(0) - 1
    pad = DN_CONV // 2
    edge = _CONV_HALO
    assert seq_ctx == _CONV_SUB
    row_e = lax.broadcasted_iota(jnp.int32, (edge, 1), 0)

    def taps(start, nrows, ok_fn):
        acc = jnp.zeros((nrows, width), F32)
        for k in range(DN_CONV):
            xs = xp_ref[_CONV_HALO + start + k - pad:_CONV_HALO + start + k - pad + nrows, :]
            if ok_fn is not None:
                xs = jnp.where(ok_fn(k - pad), xs, 0.0)
            acc = acc + xs * w_ref[k:k + 1, :]
        return acc

    assert x_ref.dtype == BF16
    r_i = lax.broadcasted_iota(jnp.int32, (_CONV_SUB, _CONV_SUB), 0)
    c_i = lax.broadcasted_iota(jnp.int32, (_CONV_SUB, _CONV_SUB), 1)
    off_taps = [k for k in range(DN_CONV) if k != pad]
    shift_all = jnp.concatenate([jnp.where(c_i == r_i + (k - pad), 1.0, 0.0).astype(BF16) for k in off_taps], axis=0)

    def interior(r0):
        shifted = _dot(shift_all, x_ref[r0:r0 + _CONV_SUB, :])
        acc = xp_ref[_CONV_HALO + r0:_CONV_HALO + r0 + _CONV_SUB, :] * w_ref[pad:pad + 1, :]
        for n, k in enumerate(off_taps):
            acc = acc + shifted[n * _CONV_SUB:(n + 1) * _CONV_SUB] * w_ref[k:k + 1, :]
        return acc

    not_ctx = jnp.logical_not(is_ctx)
    for t in range(seq // _CONV_SUB):
        r0 = t * _CONV_SUB
        mid = interior(r0)
        top = taps(r0, edge, lambda sh: not_ctx | (row_e + sh >= 0))
        bot = taps(r0 + _CONV_SUB - edge, edge, lambda sh: not_ctx | (row_e + sh < edge))
        acc = jnp.concatenate([top, mid[edge:_CONV_SUB - edge], bot], axis=0)
        y = _silu(acc)
        for hd in range(width // DN_HEAD_DIM):
            cs = slice(hd * DN_HEAD_DIM, (hd + 1) * DN_HEAD_DIM)
            yh = y[:, cs]
            inv = lax.rsqrt(jnp.sum(yh * yh, axis=-1, keepdims=True) + RMS_EPS)
            fac = jnp.where(j == 0, inv * DN_HEAD_DIM ** -0.5, jnp.where(j == 1, inv, 1.0))
            o_ref[r0:r0 + _CONV_SUB, cs] = (yh * fac).astype(o_ref.dtype)


def _delta_conv(qkv, conv_w, layer, seq_ctx):
    ns, seq, _ = qkv.shape
    blk = pl.BlockSpec((None, seq, DN_WIDTH), lambda s, j: (s, 0, j))
    return pl.pallas_call(
        functools.partial(_delta_conv_kernel, seq_ctx=seq_ctx),
        grid=(ns, 3),
        in_specs=[blk, pl.BlockSpec((None, DN_CONV, DN_WIDTH), lambda s, j: (layer, 0, j))],
        out_specs=blk,
        out_shape=jax.ShapeDtypeStruct((ns, seq, 3 * DN_WIDTH), BF16),
        scratch_shapes=[pltpu.VMEM((seq + 2 * _CONV_HALO, DN_WIDTH), F32)],
        compiler_params=_cparams(2),
        name="delta_conv",
    )(qkv, conv_w)


def _delta_step(views, s_ref, i, nblk):
    nh2 = 2 * DN_HEADS
    n = DN_BLOCK
    ids = [(g, d, hd) for g in range(len(views)) for d in range(2) for hd in range(DN_HEADS)]
    bis = [i, nblk - 1 - i]
    ablk = [[vw[3](bi) for bi in bis] for vw in views]
    atblk = [[vw[4](bi) for bi in bis] for vw in views]

    r_i = lax.broadcasted_iota(jnp.int32, (n, n), 0)
    c_i = lax.broadcasted_iota(jnp.int32, (n, n), 1)
    x_i = r_i ^ c_i
    eye = (r_i == c_i).astype(F32)
    incl = [r_i >= c_i, r_i <= c_i]
    strict = [r_i > c_i, r_i < c_i]

    def col(g, d, hd, grp):
        c = grp * nh2 + d * DN_HEADS + hd
        return ablk[g][d][:, c:c + 1]

    def row(g, d, hd, grp):
        c = grp * nh2 + d * DN_HEADS + hd
        return atblk[g][d][c:c + 1, :]

    q = [views[g][0](bis[d], hd).astype(F32) for g, d, hd in ids]
    k = [views[g][1](bis[d], hd).astype(F32) for g, d, hd in ids]
    v = [views[g][2](bis[d], hd).astype(F32) for g, d, hd in ids]
    gam_c = [col(g, d, hd, 0) for g, d, hd in ids]
    beta_c = [col(g, d, hd, 1) for g, d, hd in ids]
    tot_c = [col(g, d, hd, 2) for g, d, hd in ids]
    decay = [jnp.where(incl[d], jnp.exp(jnp.where(incl[d], col(g, d, hd, 0) - row(g, d, hd, 0), 0.0)), 0.0)
             for g, d, hd in ids]
    kb = [kj * bj for kj, bj in zip(k, beta_c)]
    kf = [kj.astype(BF16) for kj in k]
    kk = [_dot_nt(kbj.astype(BF16), kfj) for kbj, kfj in zip(kb, kf)]
    qk = [_dot_nt(qj.astype(BF16), kfj) for qj, kfj in zip(q, kf)]
    m = [jnp.where(strict[d], kkj * dj, 0.0) for (_, d, _), kkj, dj in zip(ids, kk, decay)]
    qkd = [(qkj * dj).astype(BF16) for qkj, dj in zip(qk, decay)]

    t = [eye - jnp.where(x_i < 2, mj, 0.0) for mj in m]
    s = 2
    while s < n:
        sel = (x_i >= s) & (x_i < 2 * s)
        tb = [tj.astype(BF16) for tj in t]
        y = [_dot(jnp.where(sel, mj, 0.0).astype(BF16), tbj).astype(BF16) for mj, tbj in zip(m, tb)]
        z = [_dot(tbj, yj) for tbj, yj in zip(tb, y)]
        t = [tj - zj for tj, zj in zip(t, z)]
        s *= 2

    rhs = [jnp.concatenate([vj * bj, kbj * jnp.exp(gj)], axis=1).astype(BF16)
           for vj, bj, kbj, gj in zip(v, beta_c, kb, gam_c)]
    uw = [_dot(tj.astype(BF16), rj) for tj, rj in zip(t, rhs)]
    kd_t = [(kj * jnp.exp(tc - gj)).T.astype(BF16) for kj, tc, gj in zip(k, tot_c, gam_c)]

    s_prev = [s_ref[(g * 2 + d) * DN_HEADS + hd] for g, d, hd in ids]
    lhs = [jnp.concatenate([uwj[:, DN_HEAD_DIM:], qj * jnp.exp(gj)], axis=0).astype(BF16)
           for uwj, qj, gj in zip(uw, q, gam_c)]
    ws_qs = [_dot(lj, sj.astype(BF16)) for lj, sj in zip(lhs, s_prev)]
    vnb = [(uwj[:, :DN_HEAD_DIM] - wq[:n]).astype(BF16) for uwj, wq in zip(uw, ws_qs)]
    o = [wq[n:] + _dot(qkj, vj) for wq, qkj, vj in zip(ws_qs, qkd, vnb)]
    s_new = [sj * jnp.exp(row(g, d, hd, 2)) + _dot(kdj, vj)
             for (g, d, hd), sj, kdj, vj in zip(ids, s_prev, kd_t, vnb)]
    for (g, d, hd), oj, sj in zip(ids, o, s_new):
        s_ref[(g * 2 + d) * DN_HEADS + hd] = sj
        views[g][5](bis[d], hd, oj)


def _delta_kernel(qx_ref, kx_ref, vx_ref, ax_ref, atx_ref, qc_ref, kc_ref, vc_ref, ac_ref, atc_ref,
                  ox_ref, oc_ref, s_ref):
    s_ref[...] = jnp.zeros(s_ref.shape, F32)
    ox_ref[...] = jnp.zeros(ox_ref.shape, ox_ref.dtype)
    oc_ref[...] = jnp.zeros(oc_ref.shape, oc_ref.dtype)
    n = DN_BLOCK
    group, seq_ctx = oc_ref.shape[0], oc_ref.shape[1]

    def cols(hd):
        return slice(hd * DN_HEAD_DIM, (hd + 1) * DN_HEAD_DIM)

    def rows(bi, base=0):
        return pl.ds(pl.multiple_of(base + bi * n, n), n)

    def latent_view(g):
        def add(bi, hd, val):
            cur = ox_ref[g, rows(bi), cols(hd)].astype(F32)
            ox_ref[g, rows(bi), cols(hd)] = (cur + val).astype(ox_ref.dtype)
        return (lambda bi, hd: qx_ref[g, rows(bi), cols(hd)], lambda bi, hd: kx_ref[g, rows(bi), cols(hd)],
                lambda bi, hd: vx_ref[g, rows(bi), cols(hd)], lambda bi: ax_ref[g, rows(bi), :],
                lambda bi: atx_ref[g, bi], add)

    def context_view(g):
        base = g * seq_ctx

        def add(bi, hd, val):
            cur = oc_ref[g, rows(bi), cols(hd)].astype(F32)
            oc_ref[g, rows(bi), cols(hd)] = (cur + val).astype(oc_ref.dtype)
        return (lambda bi, hd: qc_ref[rows(bi, base), cols(hd)], lambda bi, hd: kc_ref[rows(bi, base), cols(hd)],
                lambda bi, hd: vc_ref[rows(bi, base), cols(hd)], lambda bi: ac_ref[rows(bi, base), :],
                lambda bi: atc_ref[base // n + bi], add)

    def scan(views, nblk):
        def body(i, carry):
            _delta_step(views, s_ref, i, nblk)
            return carry
        lax.fori_loop(0, nblk, body, 0)

    scan([context_view(g) for g in range(group)], seq_ctx // n)
    scan([latent_view(g) for g in range(group)], ox_ref.shape[1] // n)


DN_GROUP = 2


def _delta_rule(qkvn, aux, aux_t, batch, seq_ctx):
    ns, seq, _ = qkvn.shape
    nblk = seq // DN_BLOCK
    nh2 = 2 * DN_HEADS
    grp = DN_GROUP if batch % DN_GROUP == 0 else 1
    one = pl.Buffered(1)

    def lat(j):
        return pl.BlockSpec((grp, seq, DN_WIDTH), lambda b: (b, 0, j), pipeline_mode=one)

    def ctx(j):
        return pl.BlockSpec((None, grp * seq_ctx, DN_WIDTH), lambda b: (batch, b, j))

    return pl.pallas_call(
        _delta_kernel,
        grid=(batch // grp,),
        in_specs=[lat(0), lat(1), lat(2),
                  pl.BlockSpec((grp, seq, AUX_W), lambda b: (b, 0, 0), pipeline_mode=one),
                  pl.BlockSpec((grp, nblk, 4 * nh2, DN_BLOCK), lambda b: (b, 0, 0, 0), pipeline_mode=one),
                  ctx(0), ctx(1), ctx(2),
                  pl.BlockSpec((None, grp * seq_ctx, AUX_W), lambda b: (batch, b, 0)),
                  pl.BlockSpec((None, grp * seq_ctx // DN_BLOCK, 4 * nh2, DN_BLOCK), lambda b: (batch, b, 0, 0))],
        out_specs=[pl.BlockSpec((grp, seq, DN_WIDTH), lambda b: (b, 0, 0)),
                   pl.BlockSpec((grp, seq_ctx, DN_WIDTH), lambda b: (b, 0, 0))],
        out_shape=[jax.ShapeDtypeStruct((batch, seq, DN_WIDTH), F32),
                   jax.ShapeDtypeStruct((batch, seq_ctx, DN_WIDTH), F32)],
        scratch_shapes=[pltpu.VMEM((grp * nh2, DN_HEAD_DIM, DN_HEAD_DIM), F32)],
        compiler_params=_cparams(1),
        name="delta_rule",
    )(qkvn, qkvn, qkvn, aux, aux_t, qkvn, qkvn, qkvn, aux, aux_t)


def _rope_tables(seq):
    rows = seq // GRID_W
    row = jnp.repeat(jnp.arange(rows), GRID_W).astype(F32)
    col = jnp.tile(jnp.arange(GRID_W), rows).astype(F32)
    half = MLA_ROPE // 2
    inv_freq = 1.0 / (ROPE_BASE ** (jnp.arange(0, half, 2, dtype=F32) / half))
    ang_r = row[:, None] * inv_freq
    ang_c = col[:, None] * inv_freq
    ang = jnp.concatenate([ang_r, ang_r, ang_c, ang_c], axis=-1)

    def pad(t, fill):
        return jnp.concatenate([jnp.full((seq, MLA_NOPE), fill, F32), t,
                                jnp.full((seq, HEAD_PAD - MLA_QK), fill, F32)], -1)

    lat = jnp.stack([pad(jnp.cos(ang), 1.0), pad(jnp.sin(ang), 0.0)])
    ident = jnp.stack([jnp.ones((seq, HEAD_PAD), F32), jnp.zeros((seq, HEAD_PAD), F32)])
    return jnp.stack([lat, ident])


def _head_norm_rope(xh, xh_rot, gain_cos, gain_sin, scale):
    ms = jnp.sum(xh * xh, axis=-1, keepdims=True) * (1.0 / MLA_QK)
    return (xh * gain_cos + xh_rot * gain_sin) * (lax.rsqrt(ms + RMS_EPS) * scale)


def _mla_prep_kernel(cq_ref, ckv_ref, kr_ref, krr_ref, rope_ref, qn_ref, wuq_ref, wuqr_ref, kvn_ref, wuk_ref,
                     wuv_ref, gq_ref, gqr_ref, gk_ref, gkr_ref, q_ref, k_ref, v_ref):
    cos, sin = rope_ref[0], rope_ref[1]
    gq_cos, gq_sin = gq_ref[...] * cos, gqr_ref[...] * sin
    gk_cos, gk_sin = gk_ref[...] * cos, gkr_ref[...] * sin
    cq = cq_ref[...]
    cqn = (cq * lax.rsqrt(jnp.mean(cq * cq, axis=-1, keepdims=True) + RMS_EPS) * qn_ref[...]).astype(BF16)
    q = _dot(cqn, wuq_ref[...])
    q_rot = _dot(cqn, wuqr_ref[...])
    ckv = ckv_ref[...]
    ckvn = (ckv * lax.rsqrt(jnp.mean(ckv * ckv, axis=-1, keepdims=True) + RMS_EPS) * kvn_ref[...]).astype(BF16)
    kn = _dot(ckvn, wuk_ref[...])
    lane = lax.broadcasted_iota(jnp.int32, (1, MLA_HEADS * HEAD_PAD), 1)
    ones_pad = jnp.where(lane % HEAD_PAD >= MLA_V, 1.0, 0.0)
    v_ref[...] = (_dot(ckvn, wuv_ref[...]) + ones_pad).astype(v_ref.dtype)
    kr = kr_ref[...]
    kr_rot_sin = krr_ref[...] * gk_sin
    for hd in range(MLA_HEADS):
        cs = slice(hd * HEAD_PAD, (hd + 1) * HEAD_PAD)
        qh = _head_norm_rope(q[:, cs], q_rot[:, cs], gq_cos, gq_sin, MLA_QK ** -0.5 * _LOG2E)
        q_ref[:, cs] = qh.astype(q_ref.dtype)
        xk = kn[:, cs] + kr
        ms = jnp.sum(xk * xk, axis=-1, keepdims=True) * (1.0 / MLA_QK)
        k_ref[:, cs] = ((xk * gk_cos + kr_rot_sin) * lax.rsqrt(ms + RMS_EPS)).astype(k_ref.dtype)


def _pad_heads(w, valid, lo=0):
    lead = w.shape[:-1]
    w = w.reshape(lead + (MLA_HEADS, valid))
    parts = []
    if lo:
        parts.append(jnp.zeros(lead + (MLA_HEADS, lo), w.dtype))
    parts.append(w)
    parts.append(jnp.zeros(lead + (MLA_HEADS, HEAD_PAD - lo - valid), w.dtype))
    return jnp.concatenate(parts, -1).reshape(lead + (MLA_HEADS * HEAD_PAD,))


def _pad_gain(g):
    return jnp.concatenate([g, jnp.zeros(g.shape[:-1] + (HEAD_PAD - MLA_QK,), g.dtype)], -1)[:, None, :]


def _pad_rot_gain(g):
    lead = g.shape[:-1]
    return jnp.concatenate([jnp.zeros(lead + (MLA_NOPE,), g.dtype), _rot_half_gain(g[..., MLA_NOPE:]),
                            jnp.zeros(lead + (HEAD_PAD - MLA_QK,), g.dtype)], -1)[:, None, :]


def _mla_prep(cq, ckv, kr, krr, rope, q_norm, w_uq_p, w_uq_rot, kv_norm, w_uk_p, w_uv, gq, gq_rot, gk, gk_rot,
              layer):
    ns, seq, _ = cq.shape
    tm = min(ROW_TILE, seq)

    def tok(width):
        return pl.BlockSpec((None, tm, width), lambda s, t: (s, t, 0))

    wq = MLA_HEADS * HEAD_PAD
    gain = _layer_spec(layer, (1, HEAD_PAD))
    return pl.pallas_call(
        _mla_prep_kernel,
        grid=(ns, seq // tm),
        in_specs=[tok(MLA_Q_LORA), tok(MLA_KV_LORA), tok(HEAD_PAD), tok(HEAD_PAD),
                  pl.BlockSpec((None, 2, tm, HEAD_PAD), lambda s, t: (s // (ns - 1), 0, t, 0)),
                  _layer_spec(layer, (1, MLA_Q_LORA)), _layer_spec(layer, (MLA_Q_LORA, wq)),
                  _layer_spec(layer, (MLA_Q_LORA, wq)),
                  _layer_spec(layer, (1, MLA_KV_LORA)), _layer_spec(layer, (MLA_KV_LORA, wq)),
                  _layer_spec(layer, (MLA_KV_LORA, wq)),
                  gain, gain, gain, gain],
        out_specs=[tok(wq), tok(wq), tok(wq)],
        out_shape=[jax.ShapeDtypeStruct((ns, seq, wq), BF16), jax.ShapeDtypeStruct((ns, seq, wq), BF16),
                   jax.ShapeDtypeStruct((ns, seq, wq), BF16)],
        compiler_params=_cparams(2),
        name="mla_prep",
    )(cq, ckv, kr, krr, rope, q_norm.reshape(DEPTH, 1, MLA_Q_LORA), w_uq_p, w_uq_rot,
      kv_norm.reshape(DEPTH, 1, MLA_KV_LORA), w_uk_p, w_uv, gq, gq_rot, gk, gk_rot)


def _attn_kernel(*refs, n_src):
    q_ref = refs[0]
    k_refs = refs[1:1 + n_src]
    v_refs = refs[1 + n_src:1 + 2 * n_src]
    o_ref = refs[1 + 2 * n_src]
    n_heads = q_ref.shape[1] // HEAD_PAD

    def cols(hh):
        return slice(hh * HEAD_PAD, (hh + 1) * HEAD_PAD)

    def score(hh):
        return [_dot_nt(q_ref[:, cols(hh)], k_ref[:, cols(hh)]) for k_ref in k_refs]

    def probs(scores):
        m = scores[0].max(axis=-1, keepdims=True)
        for sc in scores[1:]:
            m = jnp.maximum(m, sc.max(axis=-1, keepdims=True))
        return [jnp.exp2(sc - m).astype(BF16) for sc in scores]

    def weighted(hh, ps):
        acc = _dot(ps[0], v_refs[0][:, cols(hh)])
        for p, v_ref in zip(ps[1:], v_refs[1:]):
            acc = acc + _dot(p, v_ref[:, cols(hh)])
        return acc / acc[:, MLA_V:MLA_V + 1]

    ahead = 2
    scores = {hh: score(hh) for hh in range(min(ahead, n_heads))}
    heads = []
    for hh in range(n_heads):
        ps = probs(scores.pop(hh))
        if hh + ahead < n_heads:
            scores[hh + ahead] = score(hh + ahead)
        heads.append(weighted(hh, ps))
    lane = lax.broadcasted_iota(jnp.int32, (q_ref.shape[0], HEAD_PAD), 1)
    for pr in range(n_heads // 2):
        out = jnp.where(lane < MLA_V, heads[2 * pr], pltpu.roll(heads[2 * pr + 1], MLA_V, 1))
        o_ref[:, pr * 2 * MLA_V:(pr + 1) * 2 * MLA_V] = out.astype(o_ref.dtype)


ATTN_Q_TILE = 512
ATTN_HEADS_PER_STEP = 4


def _attn_latent(q, k, v, batch, seq_ctx):
    ns, seq, _ = q.shape
    tq = min(ATTN_Q_TILE, seq)
    pair = ATTN_HEADS_PER_STEP * HEAD_PAD
    return pl.pallas_call(
        functools.partial(_attn_kernel, n_src=2),
        grid=(batch, MLA_HEADS // ATTN_HEADS_PER_STEP, seq // tq),
        in_specs=[pl.BlockSpec((None, tq, pair), lambda b, h, t: (b, t, h)),
                  pl.BlockSpec((None, seq, pair), lambda b, h, t: (b, 0, h)),
                  pl.BlockSpec((None, seq_ctx, pair), lambda b, h, t: (batch, b, h)),
                  pl.BlockSpec((None, seq, pair), lambda b, h, t: (b, 0, h)),
                  pl.BlockSpec((None, seq_ctx, pair), lambda b, h, t: (batch, b, h))],
        out_specs=pl.BlockSpec((None, tq, ATTN_HEADS_PER_STEP * MLA_V), lambda b, h, t: (b, t, h)),
        out_shape=jax.ShapeDtypeStruct((batch, seq, MLA_WIDTH), BF16),
        compiler_params=_cparams(3),
        name="attn_latent",
    )(q, k, k, v, v)


def _attn_context(q, k, v, batch, seq_ctx):
    ns, seq, _ = q.shape
    pair = ATTN_HEADS_PER_STEP * HEAD_PAD
    return pl.pallas_call(
        functools.partial(_attn_kernel, n_src=1),
        grid=(batch, MLA_HEADS // ATTN_HEADS_PER_STEP),
        in_specs=[pl.BlockSpec((None, seq_ctx, pair), lambda b, h: (batch, b, h)),
                  pl.BlockSpec((None, seq_ctx, pair), lambda b, h: (batch, b, h)),
                  pl.BlockSpec((None, seq_ctx, pair), lambda b, h: (batch, b, h))],
        out_specs=pl.BlockSpec((None, seq_ctx, ATTN_HEADS_PER_STEP * MLA_V), lambda b, h: (0, b, h)),
        out_shape=jax.ShapeDtypeStruct((1, seq, MLA_WIDTH), BF16),
        compiler_params=_cparams(2),
        name="attn_context",
    )(q, k, v)


def _merge_kernel(h_ref, mod_ref, ya_ref, ox_ref, oc_ref, z_ref, ycx_ref, ycc_ref, gate_ref, onw_ref,
                  wa_ref, wb_ref, wc_ref, wo_ref, out_ref, *, n_latent):
    d = h_ref.shape[1]
    is_latent = pl.program_id(0) < n_latent

    def gated_sum(rs):
        o = jnp.where(is_latent, ox_ref[rs, :], oc_ref[rs, :]).astype(F32)
        z = z_ref[rs, :].astype(F32)
        yb_parts = []
        for hd in range(DN_HEADS):
            cs = slice(hd * DN_HEAD_DIM, (hd + 1) * DN_HEAD_DIM)
            oh = o[:, cs]
            ohn = oh * lax.rsqrt(jnp.mean(oh * oh, axis=-1, keepdims=True) + RMS_EPS) * onw_ref[...]
            yb_parts.append(ohn * _silu(z[:, cs]))
        yb = jnp.concatenate(yb_parts, axis=1).astype(BF16)
        yc = jnp.where(is_latent, ycx_ref[rs, :], ycc_ref[rs, :])
        m = (_sigmoid(gate_ref[rs, 0:d].astype(F32)) * _dot(ya_ref[rs, :], wa_ref[...])
             + _sigmoid(gate_ref[rs, d:2 * d].astype(F32)) * _dot(yb, wb_ref[...])
             + _sigmoid(gate_ref[rs, 2 * d:3 * d].astype(F32)) * _dot(yc, wc_ref[...]))
        return m.astype(BF16)

    half = h_ref.shape[0] // 2
    parts = [slice(0, half), slice(half, 2 * half)]
    ms = [gated_sum(rs) for rs in parts]
    for rs, m in zip(parts, ms):
        out_ref[rs, :] = h_ref[rs, :] + mod_ref[5:6, :] * _dot(m, wo_ref[...])


def _merge(h, mods, ya, o_x, o_c, z, yc_x, yc_c, gate, out_norm, w_a, w_b, w_c, w_o, layer, n_samples):
    ns, seq, d = h.shape
    tm = min(ROW_TILE, seq)

    def tok(width):
        return pl.BlockSpec((None, tm, width), lambda s, t: (s, t, 0))

    batch = o_x.shape[0]
    last_t = seq // tm - 1
    ox_spec = pl.BlockSpec((None, tm, DN_WIDTH),
                           lambda s, t: (jnp.minimum(s, batch - 1), jnp.where(s < batch, t, last_t), 0))
    oc_spec = pl.BlockSpec((None, tm, DN_WIDTH), lambda s, t: (0, jnp.where(s < batch, 0, t), 0))
    if yc_c is None:
        yc_c = yc_x
    return pl.pallas_call(
        functools.partial(_merge_kernel, n_latent=batch),
        grid=(n_samples, seq // tm),
        in_specs=[tok(d), _mod_spec(layer), tok(GM_WIDTH), ox_spec, oc_spec, tok(DN_WIDTH), ox_spec, oc_spec,
                  tok(N_BRANCH * d), _layer_spec(layer, (1, DN_HEAD_DIM)),
                  _layer_spec(layer, (GM_WIDTH, d)), _layer_spec(layer, (DN_WIDTH, d)),
                  _layer_spec(layer, (MLA_WIDTH, d)), _layer_spec(layer, (d, d))],
        out_specs=tok(d),
        out_shape=jax.ShapeDtypeStruct((n_samples, seq, d), F32),
        compiler_params=_cparams(2),
        name="merge",
    )(h, mods, ya, o_x, o_c.reshape(1, seq, DN_WIDTH), z, yc_x, yc_c, gate, out_norm.reshape(DEPTH, 1, DN_HEAD_DIM),
      w_a, w_b, w_c, w_o)


def kernel(x, c, ctx, c_ctx, ada_w, ada_b, ffn1_norm, ffn1_w_gu, ffn1_w_down, mix_norm, w_in, gm_ln, gm_ws, gm_bs, dn_conv, dn_a_log, dn_dt_bias, dn_out_norm, mla_q_norm, mla_w_uq, mla_kv_norm, mla_w_ukv, mla_qk_norm_q, mla_qk_norm_k, w_branch_gm, w_branch_dn, w_branch_mla, w_out, ffn2_norm, ffn2_w_gu, ffn2_w_down):
    batch, seq, d = x.shape
    seq_ctx = ctx.shape[1]
    assert d == D_MODEL and batch * seq_ctx == seq and batch + 1 <= MOD_ROWS
    assert seq % ROW_TILE == 0 or seq < ROW_TILE
    ns = batch + 1

    cond = jnp.concatenate([c, c_ctx[None, :], jnp.zeros((MOD_ROWS - ns, d), F32)], 0)
    mods = _ada_table(cond, ada_w, ada_b)
    h = None

    bf = lambda w: w.astype(BF16)
    ffn1_gu, ffn1_dn, ffn2_gu, ffn2_dn = bf(ffn1_w_gu), bf(ffn1_w_down), bf(ffn2_w_gu), bf(ffn2_w_down)
    w_in_p = _pack_w_in(w_in)
    gm_ws_b = bf(gm_ws)
    gm_bs_t = jnp.swapaxes(gm_bs, 1, 2)
    w_uq_p = bf(_pad_heads(mla_w_uq, MLA_QK))
    ukv = mla_w_ukv.reshape(DEPTH, MLA_KV_LORA, MLA_HEADS, MLA_NOPE + MLA_V)
    w_uk_p = bf(_pad_heads(ukv[..., :MLA_NOPE].reshape(DEPTH, MLA_KV_LORA, MLA_HEADS * MLA_NOPE), MLA_NOPE))
    w_uv = bf(_pad_heads(ukv[..., MLA_NOPE:].reshape(DEPTH, MLA_KV_LORA, MLA_WIDTH), MLA_V))
    uq_rope = mla_w_uq.reshape(DEPTH, MLA_Q_LORA, MLA_HEADS, MLA_QK)[..., MLA_NOPE:]
    w_uq_rot = bf(_pad_heads(_rot_half_cols(uq_rope).reshape(DEPTH, MLA_Q_LORA, MLA_HEADS * MLA_ROPE),
                             MLA_ROPE, lo=MLA_NOPE))
    gq, gk = _pad_gain(mla_qk_norm_q), _pad_gain(mla_qk_norm_k)
    gq_rot, gk_rot = _pad_rot_gain(mla_qk_norm_q), _pad_rot_gain(mla_qk_norm_k)
    w_a, w_b, w_c, w_o = bf(w_branch_gm), bf(w_branch_dn), bf(w_branch_mla), bf(w_out)
    rope = _rope_tables(seq)

    for l in range(DEPTH):
        need_ctx = l < DEPTH - 1
        if l == 0:
            h = _ffn_half(x, mods, ffn1_norm, ffn1_gu, ffn1_dn, l, 0, ns, ctx=ctx.reshape(1, seq, d))
        else:
            h = _ffn_half(h, mods, ffn1_norm, ffn1_gu, ffn1_dn, l, 0, ns)
        ya, qkv, z, cq, ckv, gate, ab, kr, krr = _mixer_in(h, mods, mix_norm, w_in_p, gm_ln, gm_ws_b, gm_bs_t, l)
        aux, aux_t = _delta_aux(ab, dn_a_log, dn_dt_bias, l)
        qkvn = _delta_conv(qkv, dn_conv, l, seq_ctx)
        o_x, o_c = _delta_rule(qkvn, aux, aux_t, batch, seq_ctx)
        qh, kh, vh = _mla_prep(cq, ckv, kr, krr, rope, mla_q_norm, w_uq_p, w_uq_rot, mla_kv_norm, w_uk_p, w_uv,
                               gq, gq_rot, gk, gk_rot, l)
        yc_x = _attn_latent(qh, kh, vh, batch, seq_ctx)
        n_out = ns if need_ctx else batch
        yc_c = _attn_context(qh, kh, vh, batch, seq_ctx) if need_ctx else None
        h = _merge(h, mods, ya, o_x, o_c, z, yc_x, yc_c, gate, dn_out_norm, w_a, w_b, w_c, w_o, l, n_out)
        h = _ffn_half(h, mods, ffn2_norm, ffn2_gu, ffn2_dn, l, 6, n_out)
    return h[:batch]
```
